```python
import math
import jax, jax.numpy as jnp
from jax import lax
import numpy as np

D_MODEL = 2048
BATCH = 1
SEQ = 8192
DEPTH = 2

HEAD_DIM = 128
N_GROUPS = 4
GROUP_WIDTH = D_MODEL // N_GROUPS
GROUP_HEADS = GROUP_WIDTH // HEAD_DIM
MIX_WIDTH = N_GROUPS * GROUP_WIDTH
ROPE_THETA = 500000.0
ROT_FRAC = 4
NORM_EPS = 1e-6
CONV_WIDTH = 31
SGU_CHUNK = 128
NSA_CMP_LEN = 32
NSA_CMP_STRIDE = 16
NSA_SEL_LEN = 64
NSA_SEL_TOP = 16
NSA_WINDOW = 512
NSA_N_BRANCH = 3
DSA_IDX_HEADS = 8
DSA_IDX_DIM = 64
DSA_TOPK_MAX = 256
PEER_HEADS = 8
PEER_NKEYS = 128
PEER_DKEY = 128
PEER_TOPK = 16
PEER_EXPERTS = PEER_NKEYS * PEER_NKEYS
Q_BLOCK = 128
NEG = -1e30

CONV_COLS = 2 * GROUP_WIDTH
SGU_COLS = 2 * GROUP_WIDTH
NSA_COLS = GROUP_WIDTH + 2 * NSA_N_BRANCH * HEAD_DIM + NSA_N_BRANCH * GROUP_HEADS
DSA_COLS = GROUP_WIDTH + 2 * HEAD_DIM + DSA_IDX_HEADS * DSA_IDX_DIM + DSA_IDX_DIM + DSA_IDX_HEADS
IN_COLS = CONV_COLS + SGU_COLS + NSA_COLS + DSA_COLS
IN_SPLITS = [CONV_COLS, CONV_COLS + SGU_COLS, CONV_COLS + SGU_COLS + NSA_COLS]
NSA_SPLITS = [GROUP_WIDTH, GROUP_WIDTH + 2 * NSA_N_BRANCH * HEAD_DIM]
DSA_SPLITS = [GROUP_WIDTH, GROUP_WIDTH + HEAD_DIM, GROUP_WIDTH + 2 * HEAD_DIM,
              GROUP_WIDTH + 2 * HEAD_DIM + DSA_IDX_HEADS * DSA_IDX_DIM,
              GROUP_WIDTH + 2 * HEAD_DIM + DSA_IDX_HEADS * DSA_IDX_DIM + DSA_IDX_DIM]

kernel_name = "hybrid_conv_sgu_nsa_dsa_peer"


def rms_norm(x, g):
    x32 = x.astype(jnp.float32)
    y = x32 * lax.rsqrt(jnp.mean(x32 * x32, axis=-1, keepdims=True) + NORM_EPS)
    return y.astype(x.dtype) * g


def layer_norm(x, g, b):
    x32 = x.astype(jnp.float32)
    mu = jnp.mean(x32, axis=-1, keepdims=True)
    var = jnp.mean(jnp.square(x32 - mu), axis=-1, keepdims=True)
    return ((x32 - mu) * lax.rsqrt(var + NORM_EPS)).astype(x.dtype) * g + b


def rope_tables(positions, rot_dim):
    half = rot_dim // 2
    inv = ROPE_THETA ** (-jnp.arange(half, dtype=jnp.float32) / half)
    ang = positions.astype(jnp.float32)[..., None] * inv
    return jnp.cos(ang), jnp.sin(ang)


def apply_rope(x, cos, sin):
    half = cos.shape[-1]
    if x.ndim == 4:
        cos, sin = cos[:, :, None], sin[:, :, None]
    c = cos.astype(x.dtype)
    s = sin.astype(x.dtype)
    x1 = x[..., :half]
    x2 = x[..., half:2 * half]
    return jnp.concatenate([x1 * c - x2 * s, x2 * c + x1 * s, x[..., 2 * half:]], axis=-1)


def masked_softmax(s, mask):
    p = jax.nn.softmax(jnp.where(mask, s.astype(jnp.float32), NEG), axis=-1)
    return p * mask


def to_blocks(a):
    b, t = a.shape[:2]
    return jnp.moveaxis(a.reshape((b, t // Q_BLOCK, Q_BLOCK) + a.shape[2:]), 1, 0)


def from_blocks(a):
    a = jnp.moveaxis(a, 0, 1)
    return a.reshape((a.shape[0], a.shape[1] * a.shape[2]) + a.shape[3:])


def conv_module(h, conv_w, conv_b, ln_g, ln_b):
    a, g = jnp.split(h, 2, axis=-1)
    z = a * jax.nn.sigmoid(g)
    z = lax.conv_general_dilated(z, conv_w[:, None, :], window_strides=(1,),
                                 padding=[(CONV_WIDTH - 1, 0)],
                                 dimension_numbers=('NWC', 'WIO', 'NWC'),
                                 feature_group_count=GROUP_WIDTH) + conv_b
    return jax.nn.silu(layer_norm(z, ln_g, ln_b))


def sgu_module(h, ln_g, ln_b, w_s, b_s):
    u, v = jnp.split(h, 2, axis=-1)
    v = layer_norm(v, ln_g, ln_b)
    b_, t = v.shape[:2]
    v = v.reshape(b_, t // SGU_CHUNK, SGU_CHUNK, GROUP_HEADS, HEAD_DIM)
    causal = jnp.tril(jnp.ones((SGU_CHUNK, SGU_CHUNK), dtype=bool))
    w = jnp.where(causal, w_s, 0.0)
    s = jnp.einsum('hij,bnjhd->bnihd', w, v) + b_s.T[None, None, :, :, None]
    return u * s.reshape(b_, t, GROUP_WIDTH)


def nsa_module(h, cos, sin, q_g, k_g, cmp_pos, cmp_w1, cmp_w2):
    b_, t = h.shape[:2]
    scale = HEAD_DIM ** -0.5
    q, kv, gates = jnp.split(h, NSA_SPLITS, axis=-1)
    q = rms_norm(q.reshape(b_, t, GROUP_HEADS, HEAD_DIM), q_g)
    k_cmp, v_cmp, k_slc, v_slc, k_win, v_win = jnp.split(kv, 2 * NSA_N_BRANCH, axis=-1)
    gates = jax.nn.sigmoid(gates.astype(jnp.float32)).astype(h.dtype)
    gates = gates.reshape(b_, t, GROUP_HEADS, NSA_N_BRANCH)
    t_pos = jnp.arange(t)

    n_cmp = (t - NSA_CMP_LEN) // NSA_CMP_STRIDE + 1
    cmp_start = jnp.arange(n_cmp) * NSA_CMP_STRIDE
    tok = cmp_start[:, None] + jnp.arange(NSA_CMP_LEN)[None, :]

    def compress(z, pos, w1, w2):
        blk = (z[:, tok] + pos).reshape(b_, n_cmp, NSA_CMP_LEN * HEAD_DIM)
        return jax.nn.gelu(blk @ w1) @ w2

    kc = rms_norm(compress(k_cmp, cmp_pos[0], cmp_w1[0], cmp_w2[0]), k_g[0])
    vc = compress(v_cmp, cmp_pos[1], cmp_w1[1], cmp_w2[1])
    cmp_mask = (cmp_start + NSA_CMP_LEN - 1)[None, :] <= t_pos[:, None]
    p_c = masked_softmax(jnp.einsum('bthd,bnd->bhtn', q, kc) * scale, cmp_mask)
    o_cmp = jnp.einsum('bhtn,bnd->bthd', p_c.astype(vc.dtype), vc)

    n_sel = t // NSA_SEL_LEN
    sel_start = jnp.arange(n_sel) * NSA_SEL_LEN
    overlap = ((cmp_start[:, None] < sel_start[None, :] + NSA_SEL_LEN)
               & (cmp_start[:, None] + NSA_CMP_LEN > sel_start[None, :])).astype(jnp.float32)
    imp = jnp.einsum('bhtn,nj->btj', p_c, overlap)
    cur = t_pos // NSA_SEL_LEN
    j = jnp.arange(n_sel)
    forced = (j[None] == 0) | (j[None] == cur[:, None]) | (j[None] == cur[:, None] - 1)
    future = j[None] > cur[:, None]
    imp = jnp.where(forced, jnp.inf, jnp.where(future, -jnp.inf, imp))
    n_top = min(NSA_SEL_TOP, n_sel)
    _, sel_idx = lax.top_k(imp, n_top)

    q_r = apply_rope(q, cos, sin)
    ks = apply_rope(rms_norm(k_slc, k_g[1]), cos, sin)
    kw = apply_rope(rms_norm(k_win, k_g[2]), cos, sin)
    ks_blk = ks.reshape(b_, n_sel, NSA_SEL_LEN, HEAD_DIM)
    vs_blk = v_slc.reshape(b_, n_sel, NSA_SEL_LEN, HEAD_DIM)
    kw_pad = jnp.pad(kw, ((0, 0), (NSA_WINDOW, 0), (0, 0)))
    vw_pad = jnp.pad(v_win, ((0, 0), (NSA_WINDOW, 0), (0, 0)))
    win_len = NSA_WINDOW + Q_BLOCK
    n_keys_sel = n_top * NSA_SEL_LEN

    def block_fn(args):
        qb, idx, bi = args
        tq = bi * Q_BLOCK + jnp.arange(Q_BLOCK)
        kg = jax.vmap(lambda kb, ib: kb[ib])(ks_blk, idx)
        vg = jax.vmap(lambda vb, ib: vb[ib])(vs_blk, idx)
        key_pos = idx[..., None] * NSA_SEL_LEN + jnp.arange(NSA_SEL_LEN)
        m = (key_pos <= tq[None, :, None, None]).reshape(b_, Q_BLOCK, 1, n_keys_sel)
        s = jnp.einsum('bqhd,bqnsd->bqhns', qb, kg).reshape(b_, Q_BLOCK, GROUP_HEADS, n_keys_sel) * scale
        p = masked_softmax(s, m)
        o_s = jnp.einsum('bqhk,bqkd->bqhd', p.astype(vg.dtype),
                         vg.reshape(b_, Q_BLOCK, n_keys_sel, HEAD_DIM))
        start = bi * Q_BLOCK
        kwb = lax.dynamic_slice_in_dim(kw_pad, start, win_len, axis=1)
        vwb = lax.dynamic_slice_in_dim(vw_pad, start, win_len, axis=1)
        kpos = start - NSA_WINDOW + jnp.arange(win_len)
        mw = ((kpos[None, :] <= tq[:, None]) & (kpos[None, :] > tq[:, None] - NSA_WINDOW)
              & (kpos[None, :] >= 0))
        pw = masked_softmax(jnp.einsum('bqhd,bkd->bqhk', qb, kwb) * scale, mw[None, :, None, :])
        o_w = jnp.einsum('bqhk,bkd->bqhd', pw.astype(vwb.dtype), vwb)
        return o_s, o_w

    o_s, o_w = lax.map(block_fn, (to_blocks(q_r), to_blocks(sel_idx), jnp.arange(t // Q_BLOCK)))
    o = (gates[..., 0:1] * o_cmp + gates[..., 1:2] * from_blocks(o_s)
         + gates[..., 2:3] * from_blocks(o_w))
    return o.reshape(b_, t, GROUP_WIDTH)


def dsa_module(h, cos, sin, cos_i, sin_i, q_g, k_g):
    b_, t = h.shape[:2]
    scale = HEAD_DIM ** -0.5
    q, k, v, iq, ik, iw = jnp.split(h, DSA_SPLITS, axis=-1)
    q = apply_rope(rms_norm(q.reshape(b_, t, GROUP_HEADS, HEAD_DIM), q_g), cos, sin)
    k = apply_rope(rms_norm(k, k_g), cos, sin)
    iq = apply_rope(iq.reshape(b_, t, DSA_IDX_HEADS, DSA_IDX_DIM), cos_i, sin_i)
    ik = apply_rope(ik, cos_i, sin_i)
    iw = iw * (DSA_IDX_HEADS ** -0.5)
    top = min(DSA_TOPK_MAX, t // 4)
    key_pos = jnp.arange(t)

    def block_fn(args):
        qb, iqb, iwb, bi = args
        tq = bi * Q_BLOCK + jnp.arange(Q_BLOCK)
        logits = jnp.einsum('bqhd,bsd->bqhs', iqb, ik).astype(jnp.float32) * (DSA_IDX_DIM ** -0.5)
        score = jnp.einsum('bqh,bqhs->bqs', iwb.astype(jnp.float32), jax.nn.relu(logits))
        score = jnp.where(key_pos[None, :] <= tq[:, None], score, -jnp.inf)
        _, idx = lax.top_k(score, top)
        kg = jax.vmap(lambda kk, ii: kk[ii])(k, idx)
        vg = jax.vmap(lambda vv, ii: vv[ii])(v, idx)
        m = (idx <= tq[None, :, None])[:, :, None, :]
        p = masked_softmax(jnp.einsum('bqhd,bqkd->bqhk', qb, kg) * scale, m)
        return jnp.einsum('bqhk,bqkd->bqhd', p.astype(vg.dtype), vg)

    o = lax.map(block_fn, (to_blocks(q), to_blocks(iq), to_blocks(iw), jnp.arange(t // Q_BLOCK)))
    return from_blocks(o).reshape(b_, t, GROUP_WIDTH)


def peer_ffn(x, wq, subkeys, u_tab, v_tab):
    b_, t, d = x.shape
    xb = x.reshape(b_ * t // Q_BLOCK, Q_BLOCK, d)

    def block_fn(xt):
        q = (xt @ wq).reshape(Q_BLOCK, PEER_HEADS, 2, PEER_DKEY // 2)
        s = jnp.einsum('thpd,hpnd->thpn', q, subkeys).astype(jnp.float32)
        sv, si = lax.top_k(s, PEER_TOPK)
        cand = (sv[:, :, 0, :, None] + sv[:, :, 1, None, :]).reshape(Q_BLOCK, PEER_HEADS, PEER_TOPK * PEER_TOPK)
        cv, ci = lax.top_k(cand, PEER_TOPK)
        i1 = jnp.take_along_axis(si[:, :, 0], ci // PEER_TOPK, axis=-1)
        i2 = jnp.take_along_axis(si[:, :, 1], ci % PEER_TOPK, axis=-1)
        e = i1 * PEER_NKEYS + i2
        g = jax.nn.softmax(cv, axis=-1).astype(xt.dtype)
        ug = u_tab[e]
        vg = v_tab[e]
        a = jax.nn.gelu(jnp.einsum('thkd,td->thk', ug, xt)) * g
        return jnp.einsum('thk,thkd->td', a, vg)

    return lax.map(block_fn, xb).reshape(b_, t, d)


def setup_inputs(seed: int = 0) -> dict:
    key = jax.random.key(seed)
    ks = jax.random.split(key, 32)
    f32 = jnp.float32
    L = DEPTH

    def nrm(k, shape, scale):
        return jax.random.normal(k, shape, f32) * scale

    def gain(k, shape):
        return 1.0 + 0.05 * jax.random.normal(k, shape, f32)

    return {
        "x": nrm(ks[0], (BATCH, SEQ, D_MODEL), 1.0),
        "c": nrm(ks[1], (BATCH, D_MODEL), 1.0),
        "positions": jnp.broadcast_to(jnp.arange(SEQ, dtype=jnp.int32), (BATCH, SEQ)),
        "ada_w": nrm(ks[2], (L, D_MODEL, 6 * D_MODEL), 0.5 * D_MODEL ** -0.5),
        "ada_b": nrm(ks[3], (L, 6 * D_MODEL), 0.02),
        "norm1_g": gain(ks[4], (L, D_MODEL)),
        "norm2_g": gain(ks[5], (L, D_MODEL)),
        "w_in": nrm(ks[6], (L, D_MODEL, IN_COLS), D_MODEL ** -0.5),
        "w_out": nrm(ks[7], (L, MIX_WIDTH, D_MODEL), MIX_WIDTH ** -0.5),
        "conv_w": nrm(ks[8], (L, CONV_WIDTH, GROUP_WIDTH), CONV_WIDTH ** -0.5),
        "conv_b": nrm(ks[9], (L, GROUP_WIDTH), 0.02),
        "conv_ln_g": gain(ks[10], (L, GROUP_WIDTH)),
        "conv_ln_b": nrm(ks[11], (L, GROUP_WIDTH), 0.02),
        "sgu_ln_g": gain(ks[12], (L, GROUP_WIDTH)),
        "sgu_ln_b": nrm(ks[13], (L, GROUP_WIDTH), 0.02),
        "sgu_w": nrm(ks[14], (L, GROUP_HEADS, SGU_CHUNK, SGU_CHUNK), SGU_CHUNK ** -0.5),
        "sgu_b": gain(ks[15], (L, GROUP_HEADS, SGU_CHUNK)),
        "nsa_q_g": gain(ks[16], (L, HEAD_DIM)),
        "nsa_k_g": gain(ks[17], (L, NSA_N_BRANCH, HEAD_DIM)),
        "nsa_cmp_pos": nrm(ks[18], (L, 2, NSA_CMP_LEN, HEAD_DIM), 0.1),
        "nsa_cmp_w1": nrm(ks[19], (L, 2, NSA_CMP_LEN * HEAD_DIM, HEAD_DIM), (NSA_CMP_LEN * HEAD_DIM) ** -0.5),
        "nsa_cmp_w2": nrm(ks[20], (L, 2, HEAD_DIM, HEAD_DIM), HEAD_DIM ** -0.5),
        "dsa_q_g": gain(ks[21], (L, HEAD_DIM)),
        "dsa_k_g": gain(ks[22], (L, HEAD_DIM)),
        "peer_wq": nrm(ks[23], (L, D_MODEL, PEER_HEADS * PEER_DKEY), D_MODEL ** -0.5),
        "peer_subkeys": nrm(ks[24], (L, PEER_HEADS, 2, PEER_NKEYS, PEER_DKEY // 2), (PEER_DKEY // 2) ** -0.5),
        "peer_u": nrm(ks[25], (L, PEER_EXPERTS, D_MODEL), D_MODEL ** -0.5),
        "peer_v": nrm(ks[26], (L, PEER_EXPERTS, D_MODEL), 0.5),
    }


def reference(x, c, positions, ada_w, ada_b, norm1_g, norm2_g, w_in, w_out,
              conv_w, conv_b, conv_ln_g, conv_ln_b, sgu_ln_g, sgu_ln_b, sgu_w, sgu_b,
              nsa_q_g, nsa_k_g, nsa_cmp_pos, nsa_cmp_w1, nsa_cmp_w2, dsa_q_g, dsa_k_g,
              peer_wq, peer_subkeys, peer_u, peer_v):
    cos, sin = rope_tables(positions, HEAD_DIM // ROT_FRAC)
    cos_i, sin_i = rope_tables(positions, DSA_IDX_DIM // ROT_FRAC)
    c_act = jax.nn.silu(c)
    for i in range(DEPTH):
        ada = c_act @ ada_w[i] + ada_b[i]
        sh1, sc1, g1, sh2, sc2, g2 = [a[:, None, :] for a in jnp.split(ada, 6, axis=-1)]
        hn = rms_norm(x, norm1_g[i]) * (1.0 + sc1) + sh1
        proj = hn @ w_in[i]
        pa, pb, pc, pd = jnp.split(proj, IN_SPLITS, axis=-1)
        ya = conv_module(pa, conv_w[i], conv_b[i], conv_ln_g[i], conv_ln_b[i])
        yb = sgu_module(pb, sgu_ln_g[i], sgu_ln_b[i], sgu_w[i], sgu_b[i])
        yc = nsa_module(pc, cos, sin, nsa_q_g[i], nsa_k_g[i], nsa_cmp_pos[i], nsa_cmp_w1[i], nsa_cmp_w2[i])
        yd = dsa_module(pd, cos, sin, cos_i, sin_i, dsa_q_g[i], dsa_k_g[i])
        y = jnp.concatenate([ya, yb, yc, yd], axis=-1) @ w_out[i]
        x = x + g1 * y
        hn = rms_norm(x, norm2_g[i]) * (1.0 + sc2) + sh2
        x = x + g2 * peer_ffn(hn, peer_wq[i], peer_subkeys[i], peer_u[i], peer_v[i])
    return x
```

```python
import functools

import jax
import jax.numpy as jnp
from jax import lax
from jax.experimental import pallas as pl
from jax.experimental.pallas import tpu as pltpu

F32 = jnp.float32
BF16 = jnp.bfloat16
I32 = jnp.int32

D_MODEL = 2048
HEAD_DIM = 128
GROUP_WIDTH = 512
GROUP_HEADS = 4
ROPE_THETA = 500000.0
NORM_EPS = 1e-6
CONV_WIDTH = 31
SGU_CHUNK = 128
NSA_CMP_LEN = 32
NSA_CMP_STRIDE = 16
NSA_SEL_LEN = 64
NSA_SEL_TOP = 16
NSA_WINDOW = 512
DSA_IDX_HEADS = 8
DSA_IDX_DIM = 64
DSA_TOPK_MAX = 256
PEER_HEADS = 8
PEER_NKEYS = 128
PEER_DKEY = 128
PEER_TOPK = 16
NEG = -1e30

LANES = 128
SUBLANES = 8
VMEM_LIMIT = 56 * 1024 * 1024

C_CONV, C_SGU, C_NQ, C_DQ, C_IQ, C_KVA, C_KVB, C_SM = 0, 1024, 2048, 2560, 3072, 3584, 4096, 4608
N_COLS = 4864
SM_GATES, SM_IW, SM_IK = 0, 16, 128

TQ = 128
TK = 128
KB = 512
LOG2E = 1.4426950408889634


def _cparams(*sem, vmem=VMEM_LIMIT):
    return pltpu.CompilerParams(dimension_semantics=sem, vmem_limit_bytes=vmem)


def _dot(a, b):
    return jnp.dot(a, b, preferred_element_type=F32)


def _dot_nt(a, b):
    return lax.dot_general(a, b, (((1,), (1,)), ((), ())), preferred_element_type=F32)


def _rms(x, g):
    return x * lax.rsqrt(jnp.mean(x * x, axis=-1, keepdims=True) + NORM_EPS) * g


def _layer_norm(x, g, b):
    mu = jnp.mean(x, axis=-1, keepdims=True)
    d = x - mu
    var = jnp.mean(d * d, axis=-1, keepdims=True)
    return d * lax.rsqrt(var + NORM_EPS) * g + b


def _gelu_tanh(x):
    return 0.5 * x * (1.0 + jnp.tanh(0.7978845608028654 * (x + 0.044715 * (x * x * x))))


def _silu(x):
    return x * jax.nn.sigmoid(x)


def _ada_kernel(c_ref, w_ref, b_ref, o_ref):
    ca = _silu(c_ref[...]).astype(BF16)
    o_ref[...] = _dot(ca, w_ref[...].astype(BF16)) + b_ref[...]


def _ada_call(c, ada_w, ada_b):
    n_layers, _, n = ada_w.shape
    tn = 1024
    c8 = jnp.broadcast_to(c, (SUBLANES, D_MODEL))
    out = pl.pallas_call(
        _ada_kernel,
        grid=(n_layers, n // tn),
        in_specs=[pl.BlockSpec((SUBLANES, D_MODEL), lambda l, j: (0, 0)),
                  pl.BlockSpec((None, D_MODEL, tn), lambda l, j: (l, 0, j)),
                  pl.BlockSpec((None, 1, tn), lambda l, j: (l, 0, j))],
        out_specs=pl.BlockSpec((None, SUBLANES, tn), lambda l, j: (l, 0, j)),
        out_shape=jax.ShapeDtypeStruct((n_layers, SUBLANES, n), F32),
        compiler_params=_cparams("arbitrary", "arbitrary"),
        name="ada",
    )(c8, ada_w, ada_b.reshape(n_layers, 1, n))
    return out[:, 0:1, :]


def _inproj_kernel(x_ref, g_ref, sc_ref, sh_ref, w_ref, o_ref, hn_ref):
    @pl.when(pl.program_id(1) == 0)
    def _():
        hn = _rms(x_ref[...], g_ref[...]) * (1.0 + sc_ref[...]) + sh_ref[...]
        hn_ref[...] = hn.astype(BF16)

    o_ref[...] = _dot(hn_ref[...], w_ref[...])


def _inproj_call(x, g, sc, sh, w):
    t = x.shape[0]
    tm = min(512, t)
    tn = N_COLS // 2
    row = pl.BlockSpec((1, D_MODEL), lambda i, j: (0, 0))
    return pl.pallas_call(
        _inproj_kernel,
        grid=(t // tm, N_COLS // tn),
        in_specs=[pl.BlockSpec((tm, D_MODEL), lambda i, j: (i, 0)), row, row, row,
                  pl.BlockSpec((D_MODEL, tn), lambda i, j: (0, j))],
        out_specs=pl.BlockSpec((tm, tn), lambda i, j: (i, j)),
        out_shape=jax.ShapeDtypeStruct((t, N_COLS), F32),
        scratch_shapes=[pltpu.VMEM((tm, D_MODEL), BF16)],
        compiler_params=_cparams("arbitrary", "arbitrary"),
        name="inproj",
    )(x, g, sc, sh, w)


CONV_HALO = 32
CONV_ROWS = 32


def _mixer_kernel(pa_ref, ph_ref, pb_ref, cw_ref, cb_ref, clg_ref, clb_ref, slg_ref, slb_ref,
                  sw_ref, sb_ref, ya_ref, yb_ref, zext_ref):
    tb = pa_ref.shape[0]
    gw = GROUP_WIDTH
    zext_ref[CONV_HALO:, :] = pa_ref[:, :gw] * jax.nn.sigmoid(pa_ref[:, gw:])
    zh = ph_ref[:, :gw] * jax.nn.sigmoid(ph_ref[:, gw:])
    zext_ref[:CONV_HALO, :] = jnp.where(pl.program_id(0) > 0, zh, 0.0)
    off = CONV_HALO - (CONV_WIDTH - 1)
    for r in range(0, tb, CONV_ROWS):
        acc = jnp.zeros((CONV_ROWS, gw), F32)
        for k in range(CONV_WIDTH):
            acc = acc + cw_ref[k:k + 1, :] * zext_ref[r + off + k:r + off + k + CONV_ROWS, :]
        z = _layer_norm(acc + cb_ref[...], clg_ref[...], clb_ref[...])
        ya_ref[r:r + CONV_ROWS, :] = _silu(z).astype(BF16)

    ch = SGU_CHUNK
    ri = lax.broadcasted_iota(I32, (ch, ch), 0)
    ci = lax.broadcasted_iota(I32, (ch, ch), 1)
    ws = [jnp.where(ri >= ci, sw_ref[h], 0.0).astype(BF16) for h in range(GROUP_HEADS)]
    for c in range(tb // ch):
        rows = slice(c * ch, (c + 1) * ch)
        vn = _layer_norm(pb_ref[rows, gw:], slg_ref[...], slb_ref[...]).astype(BF16)
        for h in range(GROUP_HEADS):
            cols = slice(h * HEAD_DIM, (h + 1) * HEAD_DIM)
            s = _dot(ws[h], vn[:, cols]) + sb_ref[:, cols]
            yb_ref[rows, cols] = (pb_ref[rows, cols] * s).astype(BF16)


def _mixer_call(proj, cw, cb, clg, clb, slg, slb, sw, sb):
    t = proj.shape[0]
    tb = min(256, t)
    hb = tb // CONV_HALO
    row = pl.BlockSpec((1, GROUP_WIDTH), lambda i: (0, 0))
    return pl.pallas_call(
        _mixer_kernel,
        grid=(t // tb,),
        in_specs=[pl.BlockSpec((tb, 1024), lambda i: (i, C_CONV // 1024)),
                  pl.BlockSpec((CONV_HALO, 1024), lambda i: (jnp.maximum(i * hb - 1, 0), C_CONV // 1024)),
                  pl.BlockSpec((tb, 1024), lambda i: (i, C_SGU // 1024)),
                  pl.BlockSpec((CONV_HALO, GROUP_WIDTH), lambda i: (0, 0)),
                  row, row, row, row, row,
                  pl.BlockSpec((GROUP_HEADS, SGU_CHUNK, SGU_CHUNK), lambda i: (0, 0, 0)),
                  pl.BlockSpec((SGU_CHUNK, GROUP_WIDTH), lambda i: (0, 0))],
        out_specs=[pl.BlockSpec((tb, GROUP_WIDTH), lambda i: (i, 0)),
                   pl.BlockSpec((tb, GROUP_WIDTH), lambda i: (i, 0))],
        out_shape=[jax.ShapeDtypeStruct((t, GROUP_WIDTH), BF16)] * 2,
        scratch_shapes=[pltpu.VMEM((tb + CONV_HALO, GROUP_WIDTH), F32)],
        compiler_params=_cparams("arbitrary"),
        name="mixer",
    )(proj, proj, proj, cw, cb, clg, clb, slg, slb, sw, sb)


def _prep_kernel(nq_ref, dq_ref, iq_ref, kva_ref, kvb_ref, sm_ref, cs_ref, sn_ref, csi_ref, sni_ref,
                 nqg_ref, nkg_ref, dqg_ref, dkg_ref,
                 qn_o, qr_o, qd_o, iq_o, kcmp_o, vcmp_o, ks_o, vst_o, kw_o, vwt_o, kd_o, vdt_o,
                 ik_o, gt_o, iwt_o):
    tb = nq_ref.shape[0]
    cs, sn, csi, sni = cs_ref[...], sn_ref[...], csi_ref[...], sni_ref[...]
    lane = lax.broadcasted_iota(I32, (tb, LANES), 1)
    scale = HEAD_DIM ** -0.5 * LOG2E

    def rope(x):
        partner = jnp.where(lane < 16, pltpu.roll(x, LANES - 16, 1), pltpu.roll(x, 16, 1))
        return x * cs + partner * sn

    def rope_i(x):
        partner = jnp.where((lane & 63) < 8, pltpu.roll(x, LANES - 8, 1), pltpu.roll(x, 8, 1))
        return x * csi + partner * sni

    for h in range(GROUP_HEADS):
        cols = slice(h * HEAD_DIM, (h + 1) * HEAD_DIM)
        qn = _rms(nq_ref[:, cols], nqg_ref[...])
        qn_o[:, cols] = (qn * scale).astype(BF16)
        qr_o[:, cols] = (rope(qn) * scale).astype(BF16)
        qd = rope(_rms(dq_ref[:, cols], dqg_ref[...]))
        qd_o[:, cols] = (qd * scale).astype(BF16)
    for g in range(DSA_IDX_HEADS // 2):
        xi = rope_i(iq_ref[:, g * LANES:(g + 1) * LANES]) * (DSA_IDX_DIM ** -0.5)
        iq_o[2 * g] = xi[:, :DSA_IDX_DIM].astype(BF16)
        iq_o[2 * g + 1] = xi[:, DSA_IDX_DIM:].astype(BF16)

    kcmp_o[...] = kva_ref[:, 0:128].astype(BF16)
    vcmp_o[...] = kva_ref[:, 128:256].astype(BF16)
    ks_o[...] = rope(_rms(kva_ref[:, 256:384], nkg_ref[1:2, :])).astype(BF16)
    vst_o[...] = kva_ref[:, 384:512].T.astype(BF16)
    kw_o[...] = rope(_rms(kvb_ref[:, 0:128], nkg_ref[2:3, :])).astype(BF16)
    vwt_o[...] = kvb_ref[:, 128:256].T.astype(BF16)
    kd_o[...] = rope(_rms(kvb_ref[:, 256:384], dkg_ref[...])).astype(BF16)
    vdt_o[...] = kvb_ref[:, 384:512].T.astype(BF16)

    smt = sm_ref[:, 0:LANES].T
    gt_o[...] = jax.nn.sigmoid(smt[SM_GATES:SM_GATES + 16, :])
    iwt_o[...] = smt[SM_IW:SM_IW + DSA_IDX_HEADS, :] * (DSA_IDX_HEADS ** -0.5)
    ik = rope_i(sm_ref[:, SM_IK:SM_IK + LANES])
    ik_o[...] = ik[:, :DSA_IDX_DIM].astype(BF16)


def _prep_call(proj, cs, sn, csi, sni, nqg, nkg, dqg, dkg):
    t = proj.shape[0]
    tb = min(256, t)
    blk512 = lambda c: pl.BlockSpec((tb, 512), lambda i: (i, c // 512))
    tab = pl.BlockSpec((tb, LANES), lambda i: (i, 0))
    g1 = pl.BlockSpec((1, LANES), lambda i: (0, 0))
    tok = lambda w, dt: (pl.BlockSpec((tb, w), lambda i: (i, 0)), jax.ShapeDtypeStruct((t, w), dt))
    tr = lambda r, dt: (pl.BlockSpec((r, tb), lambda i: (0, i)), jax.ShapeDtypeStruct((r, t), dt))
    outs = [tok(512, BF16), tok(512, BF16), tok(512, BF16),
            (pl.BlockSpec((DSA_IDX_HEADS, tb, DSA_IDX_DIM), lambda i: (0, i, 0)),
             jax.ShapeDtypeStruct((DSA_IDX_HEADS, t, DSA_IDX_DIM), BF16)),
            tok(128, BF16), tok(128, BF16),
            tok(128, BF16), tr(128, BF16), tok(128, BF16), tr(128, BF16), tok(128, BF16), tr(128, BF16),
            tok(DSA_IDX_DIM, BF16), tr(16, F32), tr(DSA_IDX_HEADS, F32)]
    return pl.pallas_call(
        _prep_kernel,
        grid=(t // tb,),
        in_specs=[blk512(C_NQ), blk512(C_DQ), blk512(C_IQ), blk512(C_KVA), blk512(C_KVB),
                  pl.BlockSpec((tb, 256), lambda i: (i, C_SM // 256)),
                  tab, tab, tab, tab, g1,
                  pl.BlockSpec((3, LANES), lambda i: (0, 0)), g1, g1],
        out_specs=[o[0] for o in outs],
        out_shape=[o[1] for o in outs],
        compiler_params=_cparams("arbitrary"),
        name="attn_prep",
    )(proj, proj, proj, proj, proj, proj, cs, sn, csi, sni, nqg, nkg, dqg, dkg)


def _compress_kernel(zk_ref, zv_ref, w1_ref, w2_ref, pos_ref, kg_ref, kc_o, vct_o, sh_ref):
    nc = zk_ref.shape[0]
    half = NSA_CMP_STRIDE * HEAD_DIM

    def mlp(z_ref, idx):
        z = z_ref[...]
        a = _dot(z, w1_ref[idx, 0:half, :])
        b = _dot(z, w1_ref[idx, half:2 * half, :])
        pw = _dot(pos_ref[idx], w1_ref[idx])[0:1, :]
        sh_ref[0:nc, :] = b
        sh_ref[nc:nc + SUBLANES, :] = jnp.zeros((SUBLANES, HEAD_DIM), F32)
        h = _gelu_tanh(a + sh_ref[1:nc + 1, :] + pw)
        return _dot(h.astype(BF16), w2_ref[idx])

    kc_o[...] = _rms(mlp(zk_ref, 0), kg_ref[...]).astype(BF16)
    vct_o[...] = mlp(zv_ref, 1).T.astype(BF16)


def _compress_call(zk, zv, w1, w2, pos, kg):
    nc = zk.shape[0]
    full = lambda a: pl.BlockSpec(a.shape, lambda i: (0,) * a.ndim)
    return pl.pallas_call(
        _compress_kernel,
        grid=(1,),
        in_specs=[full(zk), full(zv), full(w1), full(w2), full(pos), full(kg)],
        out_specs=[pl.BlockSpec((nc, HEAD_DIM), lambda i: (0, 0)),
                   pl.BlockSpec((HEAD_DIM, nc), lambda i: (0, 0))],
        out_shape=[jax.ShapeDtypeStruct((nc, HEAD_DIM), BF16), jax.ShapeDtypeStruct((HEAD_DIM, nc), BF16)],
        scratch_shapes=[pltpu.VMEM((nc + SUBLANES, HEAD_DIM), F32)],
        compiler_params=_cparams("arbitrary"),
        name="nsa_compress",
    )(zk, zv, w1, w2, pos, kg)


def _stack_heads(q_ref):
    return jnp.concatenate([q_ref[:, h * HEAD_DIM:(h + 1) * HEAD_DIM] for h in range(GROUP_HEADS)], axis=0)


def _attend_block(s, mask, vt_blk, m_ref, l_ref, acc_ref):
    bias = jnp.where(mask, 0.0, NEG)
    ps = []
    for h in range(GROUP_HEADS):
        cols = slice(h * TQ, (h + 1) * TQ)
        sh = s[:, cols] + bias
        m_old = m_ref[:, cols]
        m_new = jnp.maximum(m_old, jnp.max(sh, axis=0, keepdims=True))
        alpha = jnp.exp2(m_old - m_new)
        p = jnp.exp2(sh - m_new)
        l_ref[:, cols] = alpha * l_ref[:, cols] + jnp.sum(p, axis=0, keepdims=True)
        m_ref[:, cols] = m_new
        acc_ref[:, cols] = alpha * acc_ref[:, cols]
        ps.append(p.astype(BF16))
    acc_ref[...] += _dot(vt_blk, jnp.concatenate(ps, axis=1))


def _init_state(m_ref, l_ref, acc_ref):
    m_ref[...] = jnp.full(m_ref.shape, NEG, F32)
    l_ref[...] = jnp.zeros(l_ref.shape, F32)
    acc_ref[...] = jnp.zeros(acc_ref.shape, F32)


def _lane_heads(row_fn):
    return jnp.concatenate([row_fn(h) for h in range(GROUP_HEADS)], axis=1)


def _write_heads(o_ref, ot):
    for h in range(GROUP_HEADS):
        o_ref[:, h * HEAD_DIM:(h + 1) * HEAD_DIM] = ot[:, h * TQ:(h + 1) * TQ].T.astype(o_ref.dtype)


def _nsa_kernel(qn_ref, qr_ref, gt_ref, kc_ref, vct_ref, ks_ref, vst_ref, kw_ref, vwt_ref, o_ref,
                ms_ref, ls_ref, accs_ref, mw_ref, lw_ref, accw_ref):
    qi = pl.program_id(0)
    start = qi * TQ
    nc = kc_ref.shape[0]
    n_sel = nc // (NSA_SEL_LEN // NSA_CMP_STRIDE)
    t_row = start + lax.broadcasted_iota(I32, (1, TQ), 1)

    q4 = _stack_heads(qn_ref)
    sc = _dot_nt(kc_ref[...], q4)
    n_col = lax.broadcasted_iota(I32, (nc, TQ), 0)
    cmask = (n_col * NSA_CMP_STRIDE + (NSA_CMP_LEN - 1)) <= t_row
    pcs = []
    psum = jnp.zeros((nc, TQ), F32)
    for h in range(GROUP_HEADS):
        sm = jnp.where(cmask, sc[:, h * TQ:(h + 1) * TQ], NEG)
        e = jnp.exp2(sm - jnp.max(sm, axis=0, keepdims=True))
        p = jnp.where(cmask, e / jnp.sum(e, axis=0, keepdims=True), 0.0)
        psum = psum + p
        pcs.append(p.astype(BF16))
    o_cmp = _dot(vct_ref[...], jnp.concatenate(pcs, axis=1))

    jr = lax.broadcasted_iota(I32, (n_sel, nc), 0)
    nr = lax.broadcasted_iota(I32, (n_sel, nc), 1)
    ovt = jnp.where((nr >= 4 * jr - 1) & (nr <= 4 * jr + 3), 1.0, 0.0).astype(BF16)
    p_hi = psum.astype(BF16)
    p_lo = (psum - p_hi.astype(F32)).astype(BF16)
    imp = _dot(ovt, p_hi) + _dot(ovt, p_lo)
    jb = lax.broadcasted_iota(I32, (n_sel, TQ), 0)
    sel_shift = NSA_SEL_LEN.bit_length() - 1
    cur = t_row >> sel_shift
    forced = (jb == 0) | (jb == cur) | (jb == cur - 1)
    v = jnp.where(forced, jnp.inf, jnp.where(jb > cur, -jnp.inf, imp))
    jbf = jb.astype(F32)
    sel = jnp.zeros((n_sel, TQ), F32)
    for _ in range(min(NSA_SEL_TOP, n_sel)):
        mx = jnp.max(v, axis=0, keepdims=True)
        first = jnp.min(jnp.where(v == mx, jbf, float(n_sel)), axis=0, keepdims=True)
        hit = jbf == first
        sel = jnp.where(hit, 1.0, sel)
        v = jnp.where(hit, -jnp.inf, v)
    sel_b = sel.astype(BF16)

    qr4 = _stack_heads(qr_ref)
    _init_state(ms_ref, ls_ref, accs_ref)
    _init_state(mw_ref, lw_ref, accw_ref)
    kb_row = lax.broadcasted_iota(I32, (KB, TQ), 0)
    e_row = lax.broadcasted_iota(I32, (KB, n_sel), 0)
    e_col = lax.broadcasted_iota(I32, (KB, n_sel), 1)

    def sel_step(b, carry):
        kb = pl.multiple_of(b * KB, KB)
        s = _dot_nt(ks_ref[pl.ds(kb, KB), :], qr4)
        expand = jnp.where(((kb + e_row) >> sel_shift) == e_col, 1.0, 0.0).astype(BF16)
        chosen = _dot(expand, sel_b)
        mask = (chosen > 0.5) & ((kb + kb_row) <= t_row)
        _attend_block(s, mask, vst_ref[:, pl.ds(kb, KB)], ms_ref, ls_ref, accs_ref)
        return carry

    lax.fori_loop(0, (qi + KB // TQ) // (KB // TQ), sel_step, 0)

    k_row = lax.broadcasted_iota(I32, (TK, TQ), 0)
    for d in range(NSA_WINDOW // TK + 1):
        kt = qi - NSA_WINDOW // TK + d
        kb = pl.multiple_of(jnp.maximum(kt, 0) * TK, TK)
        kpos = kt * TK + k_row
        mask = (kpos <= t_row) & (kpos > t_row - NSA_WINDOW) & (kpos >= 0)
        s = _dot_nt(kw_ref[pl.ds(kb, TK), :], qr4)
        _attend_block(s, mask, vwt_ref[:, pl.ds(kb, TK)], mw_ref, lw_ref, accw_ref)

    g = lambda b: _lane_heads(lambda h: gt_ref[3 * h + b:3 * h + b + 1, :])
    ot = (g(0) * o_cmp + g(1) * (accs_ref[...] / ls_ref[...]) + g(2) * (accw_ref[...] / lw_ref[...]))
    _write_heads(o_ref, ot)


def _nsa_call(qn, qr, gt, kc, vct, ks, vst, kw, vwt):
    t = qn.shape[0]
    full = lambda a: pl.BlockSpec(a.shape, lambda i: (0,) * a.ndim)
    state = [pltpu.VMEM((1, GROUP_HEADS * TQ), F32), pltpu.VMEM((1, GROUP_HEADS * TQ), F32),
             pltpu.VMEM((HEAD_DIM, GROUP_HEADS * TQ), F32)]
    return pl.pallas_call(
        _nsa_kernel,
        grid=(t // TQ,),
        in_specs=[pl.BlockSpec((TQ, GROUP_WIDTH), lambda i: (i, 0)),
                  pl.BlockSpec((TQ, GROUP_WIDTH), lambda i: (i, 0)),
                  pl.BlockSpec((16, TQ), lambda i: (0, i)),
                  full(kc), full(vct), full(ks), full(vst), full(kw), full(vwt)],
        out_specs=pl.BlockSpec((TQ, GROUP_WIDTH), lambda i: (i, 0)),
        out_shape=jax.ShapeDtypeStruct((t, GROUP_WIDTH), BF16),
        scratch_shapes=state + state,
        compiler_params=_cparams("arbitrary"),
        name="nsa_attn",
    )(qn, qr, gt, kc, vct, ks, vst, kw, vwt)


INT_MIN = -2 ** 31
KEY_NEG_INF = -2139095041
DSA_SB = 256
COUNT_WAYS = 8


def _key_to_float(k):
    return lax.bitcast_convert_type(jnp.where(k >= 0, k, k ^ 0x7FFFFFFF), F32)


def _dsa_kernel(qd_ref, iq_ref, iwt_ref, kd_ref, vdt_ref, ik_ref, o_ref,
                sc_ref, m_ref, l_ref, acc_ref, cut_ref):
    qi = pl.program_id(0)
    start = qi * TQ
    nblk = (qi + KB // TQ) // (KB // TQ)
    t_total = kd_ref.shape[0]
    top = min(DSA_TOPK_MAX, t_total // 4)
    t_row = start + lax.broadcasted_iota(I32, (1, TQ), 1)
    k_row = lax.broadcasted_iota(I32, (KB, TQ), 0)
    s_row = lax.broadcasted_iota(I32, (DSA_SB, TQ), 0)
    iq_all = iq_ref[...].reshape(DSA_IDX_HEADS * TQ, DSA_IDX_DIM)
    iw = iwt_ref[...]

    def score_step(b, carry):
        for u in range(KB // DSA_SB):
            kb = pl.multiple_of(b * KB + u * DSA_SB, DSA_SB)
            lg = _dot_nt(ik_ref[pl.ds(kb, DSA_SB), :], iq_all)
            sc = jnp.zeros((DSA_SB, TQ), F32)
            for h in range(DSA_IDX_HEADS):
                sc = sc + iw[h:h + 1, :] * jnp.maximum(lg[:, h * TQ:(h + 1) * TQ], 0.0)
            sc_ref[pl.ds(kb, DSA_SB), :] = jnp.where((kb + s_row) <= t_row, sc, -jnp.inf)
        return carry

    lax.fori_loop(0, nblk, score_step, 0)

    def count(pred):
        def body(b, acc):
            kb = pl.multiple_of(b * KB, KB)
            hit = jnp.where(pred(sc_ref[pl.ds(kb, KB), :], kb), 1.0, 0.0)
            return acc + jnp.sum(hit.reshape(-1, COUNT_WAYS, SUBLANES, TQ), axis=0)
        part = lax.fori_loop(0, nblk, body, jnp.zeros((COUNT_WAYS, SUBLANES, TQ), F32))
        return jnp.sum(part.reshape(COUNT_WAYS * SUBLANES, TQ), axis=0, keepdims=True)

    ktop = float(top)
    key = jnp.where(count(lambda x, kb: x >= 0.0) >= ktop, 0, INT_MIN).astype(I32)

    def bit_step(b, key):
        cand = key | (1 << (30 - b))
        cand_f = _key_to_float(cand)
        return jnp.where(count(lambda x, kb: x >= cand_f) >= ktop, cand, key)

    key = lax.fori_loop(0, 31, bit_step, key)
    thr = jnp.where(key <= KEY_NEG_INF, -jnp.inf, _key_to_float(key))

    n_gt = count(lambda x, kb: x > thr)
    n_eq = count(lambda x, kb: x == thr)
    need = ktop - n_gt
    cut_ref[...] = jnp.full((1, TQ), t_total, I32)

    @pl.when(jnp.max(jnp.where((n_eq > need) & (need > 0.0) & (thr > -jnp.inf), 1.0, 0.0)) > 0.5)
    def _():
        nbits = max(1, (t_total - 1).bit_length())

        def idx_step(b, c):
            cc = c + (1 << (nbits - 1 - b))
            below = count(lambda x, kb: (x == thr) & ((kb + k_row) < cc))
            return jnp.where(below < need, cc, c)

        cut_ref[...] = lax.fori_loop(0, nbits, idx_step, jnp.zeros((1, TQ), I32))

    cut = cut_ref[...]

    q4 = _stack_heads(qd_ref)
    _init_state(m_ref, l_ref, acc_ref)

    def attn_step(b, carry):
        kb = pl.multiple_of(b * KB, KB)
        kpos = kb + k_row
        x = sc_ref[pl.ds(kb, KB), :]
        mask = ((x > thr) | ((x == thr) & (kpos <= cut))) & (kpos <= t_row)
        s = _dot_nt(kd_ref[pl.ds(kb, KB), :], q4)
        _attend_block(s, mask, vdt_ref[:, pl.ds(kb, KB)], m_ref, l_ref, acc_ref)
        return carry

    lax.fori_loop(0, nblk, attn_step, 0)
    _write_heads(o_ref, acc_ref[...] / l_ref[...])


def _dsa_call(qd, iq, iwt, kd, vdt, ik):
    t = qd.shape[0]
    full = lambda a: pl.BlockSpec(a.shape, lambda i: (0,) * a.ndim)
    return pl.pallas_call(
        _dsa_kernel,
        grid=(t // TQ,),
        in_specs=[pl.BlockSpec((TQ, GROUP_WIDTH), lambda i: (i, 0)),
                  pl.BlockSpec((DSA_IDX_HEADS, TQ, DSA_IDX_DIM), lambda i: (0, i, 0)),
                  pl.BlockSpec((DSA_IDX_HEADS, TQ), lambda i: (0, i)),
                  full(kd), full(vdt), full(ik)],
        out_specs=pl.BlockSpec((TQ, GROUP_WIDTH), lambda i: (i, 0)),
        out_shape=jax.ShapeDtypeStruct((t, GROUP_WIDTH), BF16),
        scratch_shapes=[pltpu.VMEM((t, TQ), F32),
                        pltpu.VMEM((1, GROUP_HEADS * TQ), F32), pltpu.VMEM((1, GROUP_HEADS * TQ), F32),
                        pltpu.VMEM((HEAD_DIM, GROUP_HEADS * TQ), F32),
                        pltpu.VMEM((1, TQ), I32)],
        compiler_params=_cparams("arbitrary"),
        name="dsa_attn",
    )(qd, iq, iwt, kd, vdt, ik)


def _outproj_kernel(ya_ref, yb_ref, yc_ref, yd_ref, x_ref, wo_ref, g1_ref, n2_ref, sc_ref, sh_ref,
                    wq_ref, sk_ref, x1_o, hn_o, s1_o, s2_o):
    gw = GROUP_WIDTH
    y = _dot(ya_ref[...], wo_ref[0:gw, :])
    y = y + _dot(yb_ref[...], wo_ref[gw:2 * gw, :])
    y = y + _dot(yc_ref[...], wo_ref[2 * gw:3 * gw, :])
    y = y + _dot(yd_ref[...], wo_ref[3 * gw:4 * gw, :])
    x1 = x_ref[...] + g1_ref[...] * y
    x1_o[...] = x1
    hn = (_rms(x1, n2_ref[...]) * (1.0 + sc_ref[...]) + sh_ref[...]).astype(BF16)
    hn_o[...] = hn
    q = _dot(hn, wq_ref[...]).astype(BF16)
    half = PEER_DKEY // 2
    for h in range(PEER_HEADS):
        s1_o[h] = _dot_nt(sk_ref[2 * h], q[:, (2 * h) * half:(2 * h + 1) * half])
        s2_o[h] = _dot_nt(sk_ref[2 * h + 1], q[:, (2 * h + 1) * half:(2 * h + 2) * half])


def _outproj_call(ya, yb, yc, yd, x, wo, g1, n2, sc, sh, wq, sk):
    t = x.shape[0]
    tm = min(256, t)
    tokb = lambda w: pl.BlockSpec((tm, w), lambda i: (i, 0))
    row = pl.BlockSpec((1, D_MODEL), lambda i: (0, 0))
    full = lambda a: pl.BlockSpec(a.shape, lambda i: (0,) * a.ndim)
    sspec = pl.BlockSpec((PEER_HEADS, PEER_NKEYS, tm), lambda i: (0, 0, i))
    sshape = jax.ShapeDtypeStruct((PEER_HEADS, PEER_NKEYS, t), F32)
    return pl.pallas_call(
        _outproj_kernel,
        grid=(t // tm,),
        in_specs=[tokb(512), tokb(512), tokb(512), tokb(512), tokb(D_MODEL), full(wo),
                  row, row, row, row, full(wq), full(sk)],
        out_specs=[tokb(D_MODEL), tokb(D_MODEL), sspec, sspec],
        out_shape=[jax.ShapeDtypeStruct((t, D_MODEL), F32), jax.ShapeDtypeStruct((t, D_MODEL), BF16),
                   sshape, sshape],
        compiler_params=_cparams("arbitrary"),
        name="outproj",
    )(ya, yb, yc, yd, x, wo, g1, n2, sc, sh, wq, sk)


def _top_values(v, k, out_ref):
    for r in range(k):
        mx = jnp.max(v, axis=0, keepdims=True)
        out_ref[r:r + 1, :] = mx
        v = jnp.where(v == mx, -jnp.inf, v)


def _peer_select_kernel(s1_ref, s2_ref, e1_o, e2_o, thr_o, v1_ref, v2_ref):
    k = PEER_TOPK
    row8 = lax.broadcasted_iota(I32, (SUBLANES, s1_ref.shape[2]), 0)
    for h in range(PEER_HEADS):
        s1 = s1_ref[h]
        s2 = s2_ref[h]
        _top_values(s1, k, v1_ref)
        _top_values(s2, k, v2_ref)
        v1 = v1_ref[...]
        v2 = v2_ref[...]
        pieces = [v1 + v2[0:1, :], v1[0:8, :] + v2[1:2, :]]
        for b, lim in ((2, 5), (3, 4), (4, 3), (5, 2), (6, 2), (7, 2)):
            pieces.append(jnp.where(row8 < lim, v1[0:8, :] + v2[b:b + 1, :], -jnp.inf))
        pieces.append(v1[0:1, :] + v2[8:16, :])
        cand = jnp.concatenate(pieces, axis=0)
        c = cand
        total = jnp.zeros((1, cand.shape[1]), F32)
        thr = jnp.full((1, cand.shape[1]), -jnp.inf, F32)
        for _ in range(k):
            mx = jnp.max(c, axis=0, keepdims=True)
            eq = c == mx
            new_total = total + jnp.sum(jnp.where(eq, 1.0, 0.0), axis=0, keepdims=True)
            thr = jnp.where((total < k) & (new_total >= k), mx, thr)
            total = new_total
            c = jnp.where(eq, -jnp.inf, c)
        top = v1[0:1, :] + v2[0:1, :]
        z = jnp.sum(jnp.where(cand >= thr, jnp.exp(cand - top), 0.0), axis=0, keepdims=True)
        e1_o[h] = jnp.where(s1 >= v1[k - 1:k, :], jnp.exp(s1 - v1[0:1, :]) / z, 0.0)
        e2_o[h] = jnp.where(s2 >= v2[k - 1:k, :], jnp.exp(s2 - v2[0:1, :]), 0.0)
        thr_o[h:h + 1, :] = thr


def _peer_select_call(s1, s2):
    t = s1.shape[2]
    ts = min(128, t)
    sspec = pl.BlockSpec((PEER_HEADS, PEER_NKEYS, ts), lambda i: (0, 0, i))
    sshape = jax.ShapeDtypeStruct((PEER_HEADS, PEER_NKEYS, t), F32)
    return pl.pallas_call(
        _peer_select_kernel,
        grid=(t // ts,),
        in_specs=[sspec, sspec],
        out_specs=[sspec, sspec, pl.BlockSpec((PEER_HEADS, ts), lambda i: (0, i))],
        out_shape=[sshape, sshape, jax.ShapeDtypeStruct((PEER_HEADS, t), F32)],
        scratch_shapes=[pltpu.VMEM((PEER_TOPK, ts), F32), pltpu.VMEM((PEER_TOPK, ts), F32)],
        compiler_params=_cparams("arbitrary"),
        name="peer_select",
    )(s1, s2)


PEER_TT = 512
PEER_TE = 1024
PEER_SUB = 32


def _peer_kernel(hn_ref, u_ref, v_ref, s1_ref, e1_ref, s2_ref, e2_ref, thr_ref, o_ref,
                 ht0_ref, ht1_ref, a0_ref, a1_ref):
    s = pl.program_id(1)
    tt = hn_ref.shape[0]

    @pl.when(s == 0)
    def _():
        o_ref[...] = jnp.zeros(o_ref.shape, F32)
        ht1_ref[...] = jnp.zeros(ht1_ref.shape, F32)
        a0_ref[...] = jnp.zeros(a0_ref.shape, BF16)

    def step(ht_w, ht_r, a_w, a_r):
        n_chunks = PEER_TE // PEER_NKEYS
        ncol = D_MODEL // n_chunks
        for c in range(n_chunks):
            rows = slice(c * PEER_NKEYS, (c + 1) * PEER_NKEYS)
            for lt in range(tt // LANES):
                cols = slice(lt * LANES, (lt + 1) * LANES)
                halves = []
                for k2 in range(0, PEER_NKEYS, PEER_SUB):
                    w = jnp.zeros((PEER_SUB, LANES), F32)
                    for h in range(PEER_HEADS):
                        zsum = s1_ref[h, c:c + 1, cols] + s2_ref[h, k2:k2 + PEER_SUB, cols]
                        gate = e1_ref[h, c:c + 1, cols] * e2_ref[h, k2:k2 + PEER_SUB, cols]
                        w = w + jnp.where(zsum >= thr_ref[h:h + 1, cols], gate, 0.0)
                    r0 = c * PEER_NKEYS + k2
                    halves.append(_gelu_tanh(ht_r[r0:r0 + PEER_SUB, cols]) * w)
                a_w[cols, rows] = jnp.concatenate(halves, axis=0).T.astype(BF16)
            oc = slice(c * ncol, (c + 1) * ncol)
            contrib = _dot(a_r[...], v_ref[:, oc])
            o_ref[:, oc] += jnp.where(s >= 2, contrib, 0.0)
            if c % (n_chunks // 2) == 0:
                tc = slice((2 * c // n_chunks) * (tt // 2), (2 * c // n_chunks + 1) * (tt // 2))
                ht_w[:, tc] = _dot_nt(u_ref[...], hn_ref[tc, :])

    @pl.when(s % 2 == 0)
    def _():
        step(ht0_ref, ht1_ref, a1_ref, a0_ref)

    @pl.when(s % 2 == 1)
    def _():
        step(ht1_ref, ht0_ref, a0_ref, a1_ref)


def _peer_call(hn, u, v, s1, e1, s2, e2, thr):
    t = hn.shape[0]
    tt = min(PEER_TT, t)
    n_tiles = u.shape[0] // PEER_TE
    rows1 = PEER_TE // PEER_NKEYS
    tile = lambda lag: (lambda i, s: (jnp.clip(s - lag, 0, n_tiles - 1), 0))
    s1spec = pl.BlockSpec((PEER_HEADS, rows1, tt), lambda i, s: (0, jnp.clip(s - 1, 0, n_tiles - 1), i))
    s2spec = pl.BlockSpec((PEER_HEADS, PEER_NKEYS, tt), lambda i, s: (0, 0, i))
    return pl.pallas_call(
        _peer_kernel,
        grid=(t // tt, n_tiles + 2),
        in_specs=[pl.BlockSpec((tt, D_MODEL), lambda i, s: (i, 0)),
                  pl.BlockSpec((PEER_TE, D_MODEL), tile(0)),
                  pl.BlockSpec((PEER_TE, D_MODEL), tile(2)),
                  s1spec, s1spec, s2spec, s2spec,
                  pl.BlockSpec((PEER_HEADS, tt), lambda i, s: (0, i))],
        out_specs=pl.BlockSpec((tt, D_MODEL), lambda i, s: (i, 0)),
        out_shape=jax.ShapeDtypeStruct((t, D_MODEL), F32),
        scratch_shapes=[pltpu.VMEM((PEER_TE, tt), F32), pltpu.VMEM((PEER_TE, tt), F32),
                        pltpu.VMEM((tt, PEER_TE), BF16), pltpu.VMEM((tt, PEER_TE), BF16)],
        compiler_params=_cparams("arbitrary", "arbitrary"),
        name="peer_experts",
    )(hn, u, v, s1, e1, s2, e2, thr)


def _residual_kernel(x_ref, p_ref, g_ref, o_ref):
    o_ref[...] = x_ref[...] + g_ref[...] * p_ref[...]


def _residual_call(x1, p, g2):
    t = x1.shape[0]
    tb = min(512, t)
    tok = pl.BlockSpec((tb, D_MODEL), lambda i: (i, 0))
    return pl.pallas_call(
        _residual_kernel,
        grid=(t // tb,),
        in_specs=[tok, tok, pl.BlockSpec((1, D_MODEL), lambda i: (0, 0))],
        out_specs=tok,
        out_shape=jax.ShapeDtypeStruct((t, D_MODEL), F32),
        compiler_params=_cparams("arbitrary"),
        name="peer_residual",
    )(x1, p, g2)


def _rope_tables(positions, rot_dim, period):
    half = rot_dim // 2
    inv = ROPE_THETA ** (-jnp.arange(half, dtype=F32) / half)
    ang = positions.astype(F32)[:, None] * inv
    cos, sin = jnp.cos(ang), jnp.sin(ang)
    t = positions.shape[0]
    pad1 = jnp.ones((t, period - rot_dim), F32)
    pad0 = jnp.zeros((t, period - rot_dim), F32)
    cs = jnp.concatenate([cos, cos, pad1], axis=1)
    sn = jnp.concatenate([-sin, sin, pad0], axis=1)
    reps = LANES // period
    return jnp.tile(cs, (1, reps)), jnp.tile(sn, (1, reps))


def _reorder_w_in(w):
    nsa = 2048
    dsa = nsa + 512 + 768 + 12
    d = w.shape[0]
    zeros = lambda n: jnp.zeros((d, n), w.dtype)
    small = jnp.concatenate([w[:, nsa + 1280:nsa + 1292], zeros(SM_IW - 12),
                             w[:, dsa + 1344:dsa + 1352], zeros(SM_IK - SM_IW - DSA_IDX_HEADS),
                             w[:, dsa + 1280:dsa + 1344], zeros(256 - SM_IK - DSA_IDX_DIM)], axis=1)
    return jnp.concatenate([w[:, 0:2048],
                            w[:, nsa:nsa + 512],
                            w[:, dsa:dsa + 512],
                            w[:, dsa + 768:dsa + 1280],
                            w[:, nsa + 512:nsa + 1024],
                            w[:, nsa + 1024:nsa + 1280],
                            w[:, dsa + 512:dsa + 768],
                            small], axis=1).astype(BF16)


def kernel(x, c, positions, ada_w, ada_b, norm1_g, norm2_g, w_in, w_out, conv_w, conv_b, conv_ln_g, conv_ln_b,
           sgu_ln_g, sgu_ln_b, sgu_w, sgu_b, nsa_q_g, nsa_k_g, nsa_cmp_pos, nsa_cmp_w1, nsa_cmp_w2, dsa_q_g,
           dsa_k_g, peer_wq, peer_subkeys, peer_u, peer_v):
    assert x.shape[0] == 1 and c.shape[0] == 1
    n_layers = ada_w.shape[0]
    xs = x[0]
    t = xs.shape[0]
    assert t % 1024 == 0
    pos = positions[0]
    cs, sn = _rope_tables(pos, HEAD_DIM // 4, LANES)
    csi, sni = _rope_tables(pos, DSA_IDX_DIM // 4, DSA_IDX_DIM)
    ada = _ada_call(c, ada_w, ada_b)
    row = lambda v: v.reshape(1, -1)
    nc = t // NSA_CMP_STRIDE

    for i in range(n_layers):
        sh1, sc1, g1, sh2, sc2, g2 = [ada[i, :, k * D_MODEL:(k + 1) * D_MODEL] for k in range(6)]
        proj = _inproj_call(xs, row(norm1_g[i]), sc1, sh1, _reorder_w_in(w_in[i]))

        cw = jnp.concatenate([conv_w[i], jnp.zeros((CONV_HALO - CONV_WIDTH, GROUP_WIDTH), F32)], axis=0)
        sb = jnp.repeat(sgu_b[i].T, HEAD_DIM, axis=1)
        ya, yb = _mixer_call(proj, cw, row(conv_b[i]), row(conv_ln_g[i]), row(conv_ln_b[i]),
                             row(sgu_ln_g[i]), row(sgu_ln_b[i]), sgu_w[i], sb)

        (qn, qr, qd, iq, kcmp, vcmp, ks, vst, kw, vwt, kd, vdt, ik, gt, iwt) = _prep_call(
            proj, cs, sn, csi, sni, row(nsa_q_g[i]), nsa_k_g[i], row(dsa_q_g[i]), row(dsa_k_g[i]))

        posb = jnp.broadcast_to(nsa_cmp_pos[i].reshape(2, 1, NSA_CMP_LEN * HEAD_DIM),
                                (2, SUBLANES, NSA_CMP_LEN * HEAD_DIM)).astype(BF16)
        kc, vct = _compress_call(kcmp.reshape(nc, NSA_CMP_STRIDE * HEAD_DIM),
                                 vcmp.reshape(nc, NSA_CMP_STRIDE * HEAD_DIM),
                                 nsa_cmp_w1[i].astype(BF16), nsa_cmp_w2[i].astype(BF16), posb,
                                 nsa_k_g[i, 0:1])
        yc = _nsa_call(qn, qr, gt, kc, vct, ks, vst, kw, vwt)
        yd = _dsa_call(qd, iq, iwt, kd, vdt, ik)

        sk = peer_subkeys[i].reshape(2 * PEER_HEADS, PEER_NKEYS, PEER_DKEY // 2).astype(BF16)
        x1, hn2, s1, s2 = _outproj_call(ya, yb, yc, yd, xs, w_out[i].astype(BF16), g1, row(norm2_g[i]),
                                        sc2, sh2, peer_wq[i].astype(BF16), sk)
        e1, e2, thr = _peer_select_call(s1, s2)
        pe = _peer_call(hn2, peer_u[i].astype(BF16), peer_v[i].astype(BF16), s1, e1, s2, e2, thr)
        xs = _residual_call(x1, pe, g2)
    return xs[None]
```

```python
import functools

import jax
import jax.numpy as jnp
from jax import lax
from jax.experimental import pallas as pl
from jax.experimental.pallas import tpu as pltpu

F32 = jnp.float32
BF16 = jnp.bfloat16
I32 = jnp.int32

D_MODEL = 2048
HEAD_DIM = 128
GROUP_WIDTH = 512
GROUP_HEADS = 4
ROPE_THETA = 500000.0
NORM_EPS = 1e-6
CONV_WIDTH = 31
SGU_CHUNK = 128
NSA_CMP_LEN = 32
NSA_CMP_STRIDE = 16
NSA_SEL_LEN = 64
NSA_SEL_TOP = 16
NSA_WINDOW = 512
DSA_IDX_HEADS = 8
DSA_IDX_DIM = 64
DSA_TOPK_MAX = 256
PEER_HEADS = 8
PEER_NKEYS = 128
PEER_DKEY = 128
PEER_TOPK = 16
NEG = -1e30

LANES = 128
SUBLANES = 8
VMEM_LIMIT = 56 * 1024 * 1024

C_CONV, C_SGU, C_NQ, C_KV, C_TAIL = 0, 1024, 2048, 2560, 3328
IN_COLS = 4692
N_COLS = 4864
TAIL_SHIFT = 12

TQ = 128
TK = 128
KB = 512
LOG2E = 1.4426950408889634


def _cparams(*sem, vmem=VMEM_LIMIT):
    return pltpu.CompilerParams(dimension_semantics=sem, vmem_limit_bytes=vmem)


def _dot(a, b):
    return jnp.dot(a, b, preferred_element_type=F32)


def _dot_nt(a, b):
    return lax.dot_general(a, b, (((1,), (1,)), ((), ())), preferred_element_type=F32)


def _rms(x, g):
    return x * lax.rsqrt(jnp.mean(x * x, axis=-1, keepdims=True) + NORM_EPS) * g


def _layer_norm(x, g, b):
    mu = jnp.mean(x, axis=-1, keepdims=True)
    d = x - mu
    var = jnp.mean(d * d, axis=-1, keepdims=True)
    return d * lax.rsqrt(var + NORM_EPS) * g + b


def _gelu_tanh(x):
    return 0.5 * x * (1.0 + jnp.tanh(0.7978845608028654 * (x + 0.044715 * (x * x * x))))


def _silu(x):
    return x * jax.nn.sigmoid(x)


def _ada_kernel(c_ref, w_ref, b_ref, o_ref):
    ca = _silu(c_ref[...]).astype(BF16)
    o_ref[...] = _dot(ca, w_ref[...].astype(BF16)) + b_ref[...]


def _ada_call(c, ada_w, ada_b):
    n_layers, _, n = ada_w.shape
    tn = 1024
    c8 = jnp.broadcast_to(c, (SUBLANES, D_MODEL))
    out = pl.pallas_call(
        _ada_kernel,
        grid=(n_layers, n // tn),
        in_specs=[pl.BlockSpec((SUBLANES, D_MODEL), lambda l, j: (0, 0)),
                  pl.BlockSpec((None, D_MODEL, tn), lambda l, j: (l, 0, j)),
                  pl.BlockSpec((None, 1, tn), lambda l, j: (l, 0, j))],
        out_specs=pl.BlockSpec((None, SUBLANES, tn), lambda l, j: (l, 0, j)),
        out_shape=jax.ShapeDtypeStruct((n_layers, SUBLANES, n), F32),
        compiler_params=_cparams("arbitrary", "arbitrary"),
        name="ada",
    )(c8, ada_w, ada_b.reshape(n_layers, 1, n))
    return out[:, 0:1, :]


def _inproj_kernel(x_ref, g_ref, sc_ref, sh_ref, w_ref, o_ref, hn_ref):
    @pl.when(pl.program_id(1) == 0)
    def _():
        hn = _rms(x_ref[...], g_ref[...]) * (1.0 + sc_ref[...]) + sh_ref[...]
        hn_ref[...] = hn.astype(BF16)

    o_ref[...] = _dot(hn_ref[...], w_ref[...])


def _inproj_call(x, g, sc, sh, w):
    t = x.shape[0]
    tm = min(512, t)
    tn = N_COLS // 2
    row = pl.BlockSpec((1, D_MODEL), lambda i, j: (0, 0))
    return pl.pallas_call(
        _inproj_kernel,
        grid=(t // tm, N_COLS // tn),
        in_specs=[pl.BlockSpec((tm, D_MODEL), lambda i, j: (i, 0)), row, row, row,
                  pl.BlockSpec((D_MODEL, tn), lambda i, j: (0, j))],
        out_specs=pl.BlockSpec((tm, tn), lambda i, j: (i, j)),
        out_shape=jax.ShapeDtypeStruct((t, N_COLS), F32),
        scratch_shapes=[pltpu.VMEM((tm, D_MODEL), BF16)],
        compiler_params=_cparams("arbitrary", "arbitrary"),
        name="inproj",
    )(x, g, sc, sh, w)


CONV_HALO = 32
CONV_ROWS = 32


def _mixer_kernel(pa_ref, ph_ref, pb_ref, cw_ref, cb_ref, clg_ref, clb_ref, slg_ref, slb_ref,
                  sw_ref, sb_ref, ya_ref, yb_ref, zext_ref):
    tb = pa_ref.shape[0]
    gw = GROUP_WIDTH
    zext_ref[CONV_HALO:, :] = pa_ref[:, :gw] * jax.nn.sigmoid(pa_ref[:, gw:])
    zh = ph_ref[:, :gw] * jax.nn.sigmoid(ph_ref[:, gw:])
    zext_ref[:CONV_HALO, :] = jnp.where(pl.program_id(0) > 0, zh, 0.0)
    off = CONV_HALO - (CONV_WIDTH - 1)
    for r in range(0, tb, CONV_ROWS):
        acc = jnp.zeros((CONV_ROWS, gw), F32)
        for k in range(CONV_WIDTH):
            acc = acc + cw_ref[k:k + 1, :] * zext_ref[r + off + k:r + off + k + CONV_ROWS, :]
        z = _layer_norm(acc + cb_ref[...], clg_ref[...], clb_ref[...])
        ya_ref[r:r + CONV_ROWS, :] = _silu(z).astype(BF16)

    ch = SGU_CHUNK
    ri = lax.broadcasted_iota(I32, (ch, ch), 0)
    ci = lax.broadcasted_iota(I32, (ch, ch), 1)
    ws = [jnp.where(ri >= ci, sw_ref[h], 0.0).astype(BF16) for h in range(GROUP_HEADS)]
    for c in range(tb // ch):
        rows = slice(c * ch, (c + 1) * ch)
        vn = _layer_norm(pb_ref[rows, gw:], slg_ref[...], slb_ref[...]).astype(BF16)
        for h in range(GROUP_HEADS):
            cols = slice(h * HEAD_DIM, (h + 1) * HEAD_DIM)
            s = _dot(ws[h], vn[:, cols]) + sb_ref[:, cols]
            yb_ref[rows, cols] = (pb_ref[rows, cols] * s).astype(BF16)


def _mixer_call(proj, cw, cb, clg, clb, slg, slb, sw, sb):
    t = proj.shape[0]
    tb = min(256, t)
    hb = tb // CONV_HALO
    row = pl.BlockSpec((1, GROUP_WIDTH), lambda i: (0, 0))
    return pl.pallas_call(
        _mixer_kernel,
        grid=(t // tb,),
        in_specs=[pl.BlockSpec((tb, 1024), lambda i: (i, C_CONV // 1024)),
                  pl.BlockSpec((CONV_HALO, 1024), lambda i: (jnp.maximum(i * hb - 1, 0), C_CONV // 1024)),
                  pl.BlockSpec((tb, 1024), lambda i: (i, C_SGU // 1024)),
                  pl.BlockSpec((CONV_HALO, GROUP_WIDTH), lambda i: (0, 0)),
                  row, row, row, row, row,
                  pl.BlockSpec((GROUP_HEADS, SGU_CHUNK, SGU_CHUNK), lambda i: (0, 0, 0)),
                  pl.BlockSpec((SGU_CHUNK, GROUP_WIDTH), lambda i: (0, 0))],
        out_specs=[pl.BlockSpec((tb, GROUP_WIDTH), lambda i: (i, 0)),
                   pl.BlockSpec((tb, GROUP_WIDTH), lambda i: (i, 0))],
        out_shape=[jax.ShapeDtypeStruct((t, GROUP_WIDTH), BF16)] * 2,
        scratch_shapes=[pltpu.VMEM((tb + CONV_HALO, GROUP_WIDTH), F32)],
        compiler_params=_cparams("arbitrary"),
        name="mixer",
    )(proj, proj, proj, cw, cb, clg, clb, slg, slb, sw, sb)


def _prep_kernel(nq_ref, kv0_ref, kv1_ref, kv2_ref, t0_ref, t1_ref, t2_ref, t3_ref, t4_ref, t5_ref,
                 cs_ref, sn_ref, csi_ref, sni_ref, nqg_ref, nkg_ref, dqg_ref, dkg_ref,
                 qn_o, qr_o, qd_o, iq_o, kcmp_o, vcmp_o, ks_o, vst_o, kw_o, vwt_o, kd_o, vdt_o,
                 ik_o, gt_o, iwt_o):
    tb = nq_ref.shape[0]
    cs, sn, csi, sni = cs_ref[...], sn_ref[...], csi_ref[...], sni_ref[...]
    lane = lax.broadcasted_iota(I32, (tb, LANES), 1)
    scale = HEAD_DIM ** -0.5 * LOG2E

    tail = [r[:, half * LANES:(half + 1) * LANES]
            for r in (t0_ref, t1_ref, t2_ref, t3_ref, t4_ref, t5_ref) for half in range(2)]
    rolled = [pltpu.roll(c, LANES - TAIL_SHIFT, 1) for c in tail]
    dsa = [jnp.where(lane < LANES - TAIL_SHIFT, rolled[k], rolled[k + 1]) for k in range(len(tail) - 1)]

    def rope(x):
        partner = jnp.where(lane < 16, pltpu.roll(x, LANES - 16, 1), pltpu.roll(x, 16, 1))
        return x * cs + partner * sn

    def rope_i(x):
        partner = jnp.where((lane & 63) < 8, pltpu.roll(x, LANES - 8, 1), pltpu.roll(x, 8, 1))
        return x * csi + partner * sni

    for h in range(GROUP_HEADS):
        cols = slice(h * HEAD_DIM, (h + 1) * HEAD_DIM)
        qn = _rms(nq_ref[:, cols], nqg_ref[...])
        qn_o[:, cols] = (qn * scale).astype(BF16)
        qr_o[:, cols] = (rope(qn) * scale).astype(BF16)
        qd = rope(_rms(dsa[h], dqg_ref[...]))
        qd_o[:, cols] = (qd * scale).astype(BF16)
    for g in range(DSA_IDX_HEADS // 2):
        xi = rope_i(dsa[6 + g]) * (DSA_IDX_DIM ** -0.5)
        iq_o[2 * g] = xi[:, :DSA_IDX_DIM].astype(BF16)
        iq_o[2 * g + 1] = xi[:, DSA_IDX_DIM:].astype(BF16)

    kcmp_o[...] = kv0_ref[:, 0:128].astype(BF16)
    vcmp_o[...] = kv0_ref[:, 128:256].astype(BF16)
    ks_o[...] = rope(_rms(kv1_ref[:, 0:128], nkg_ref[1:2, :])).astype(BF16)
    vst_o[...] = kv1_ref[:, 128:256].T.astype(BF16)
    kw_o[...] = rope(_rms(kv2_ref[:, 0:128], nkg_ref[2:3, :])).astype(BF16)
    vwt_o[...] = kv2_ref[:, 128:256].T.astype(BF16)
    kd_o[...] = rope(_rms(dsa[4], dkg_ref[...])).astype(BF16)
    vdt_o[...] = dsa[5].T.astype(BF16)

    gt_o[...] = jax.nn.sigmoid(tail[0].T[0:16, :])
    iwt_o[...] = dsa[10].T[DSA_IDX_DIM:DSA_IDX_DIM + DSA_IDX_HEADS, :] * (DSA_IDX_HEADS ** -0.5)
    ik_o[...] = rope_i(dsa[10])[:, :DSA_IDX_DIM].astype(BF16)


def _prep_call(proj, cs, sn, csi, sni, nqg, nkg, dqg, dkg):
    t = proj.shape[0]
    tb = min(256, t)
    blk256 = lambda c: pl.BlockSpec((tb, 256), lambda i: (i, c // 256))
    tab = pl.BlockSpec((tb, LANES), lambda i: (i, 0))
    g1 = pl.BlockSpec((1, LANES), lambda i: (0, 0))
    tok = lambda w, dt: (pl.BlockSpec((tb, w), lambda i: (i, 0)), jax.ShapeDtypeStruct((t, w), dt))
    tr = lambda r, dt: (pl.BlockSpec((r, tb), lambda i: (0, i)), jax.ShapeDtypeStruct((r, t), dt))
    outs = [tok(512, BF16), tok(512, BF16), tok(512, BF16),
            (pl.BlockSpec((DSA_IDX_HEADS, tb, DSA_IDX_DIM), lambda i: (0, i, 0)),
             jax.ShapeDtypeStruct((DSA_IDX_HEADS, t, DSA_IDX_DIM), BF16)),
            tok(128, BF16), tok(128, BF16),
            tok(128, BF16), tr(128, BF16), tok(128, BF16), tr(128, BF16), tok(128, BF16), tr(128, BF16),
            tok(DSA_IDX_DIM, BF16), tr(16, F32), tr(DSA_IDX_HEADS, F32)]
    return pl.pallas_call(
        _prep_kernel,
        grid=(t // tb,),
        in_specs=[pl.BlockSpec((tb, 512), lambda i: (i, C_NQ // 512))]
                 + [blk256(C_KV + 256 * k) for k in range(3)]
                 + [blk256(C_TAIL + 256 * k) for k in range(6)]
                 + [tab, tab, tab, tab, g1, pl.BlockSpec((3, LANES), lambda i: (0, 0)), g1, g1],
        out_specs=[o[0] for o in outs],
        out_shape=[o[1] for o in outs],
        compiler_params=_cparams("arbitrary"),
        name="attn_prep",
    )(*([proj] * 10), cs, sn, csi, sni, nqg, nkg, dqg, dkg)


def _compress_kernel(zk_ref, zv_ref, w1_ref, w2_ref, pos_ref, kg_ref, kc_o, vct_o, sh_ref):
    nc = zk_ref.shape[0]
    half = NSA_CMP_STRIDE * HEAD_DIM

    def mlp(z_ref, idx):
        z = z_ref[...]
        a = _dot(z, w1_ref[idx, 0:half, :])
        b = _dot(z, w1_ref[idx, half:2 * half, :])
        pw = _dot(pos_ref[idx], w1_ref[idx])[0:1, :]
        sh_ref[0:nc, :] = b
        sh_ref[nc:nc + SUBLANES, :] = jnp.zeros((SUBLANES, HEAD_DIM), F32)
        h = _gelu_tanh(a + sh_ref[1:nc + 1, :] + pw)
        return _dot(h.astype(BF16), w2_ref[idx])

    kc_o[...] = _rms(mlp(zk_ref, 0), kg_ref[...]).astype(BF16)
    vct_o[...] = mlp(zv_ref, 1).T.astype(BF16)


def _compress_call(zk, zv, w1, w2, pos, kg):
    nc = zk.shape[0]
    full = lambda a: pl.BlockSpec(a.shape, lambda i: (0,) * a.ndim)
    return pl.pallas_call(
        _compress_kernel,
        grid=(1,),
        in_specs=[full(zk), full(zv), full(w1), full(w2), full(pos), full(kg)],
        out_specs=[pl.BlockSpec((nc, HEAD_DIM), lambda i: (0, 0)),
                   pl.BlockSpec((HEAD_DIM, nc), lambda i: (0, 0))],
        out_shape=[jax.ShapeDtypeStruct((nc, HEAD_DIM), BF16), jax.ShapeDtypeStruct((HEAD_DIM, nc), BF16)],
        scratch_shapes=[pltpu.VMEM((nc + SUBLANES, HEAD_DIM), F32)],
        compiler_params=_cparams("arbitrary"),
        name="nsa_compress",
    )(zk, zv, w1, w2, pos, kg)


def _stack_heads(q_ref):
    return jnp.concatenate([q_ref[:, h * HEAD_DIM:(h + 1) * HEAD_DIM] for h in range(GROUP_HEADS)], axis=0)


def _attend_block(s, mask, vt_blk, m_ref, l_ref, acc_ref):
    bias = jnp.where(mask, 0.0, NEG)
    ps = []
    for h in range(GROUP_HEADS):
        cols = slice(h * TQ, (h + 1) * TQ)
        sh = s[:, cols] + bias
        m_old = m_ref[:, cols]
        m_new = jnp.maximum(m_old, jnp.max(sh, axis=0, keepdims=True))
        alpha = jnp.exp2(m_old - m_new)
        p = jnp.exp2(sh - m_new)
        l_ref[:, cols] = alpha * l_ref[:, cols] + jnp.sum(p, axis=0, keepdims=True)
        m_ref[:, cols] = m_new
        acc_ref[:, cols] = alpha * acc_ref[:, cols]
        ps.append(p.astype(BF16))
    acc_ref[...] += _dot(vt_blk, jnp.concatenate(ps, axis=1))


def _init_state(m_ref, l_ref, acc_ref):
    m_ref[...] = jnp.full(m_ref.shape, NEG, F32)
    l_ref[...] = jnp.zeros(l_ref.shape, F32)
    acc_ref[...] = jnp.zeros(acc_ref.shape, F32)


def _lane_heads(row_fn):
    return jnp.concatenate([row_fn(h) for h in range(GROUP_HEADS)], axis=1)


def _write_heads(o_ref, ot):
    for h in range(GROUP_HEADS):
        o_ref[:, h * HEAD_DIM:(h + 1) * HEAD_DIM] = ot[:, h * TQ:(h + 1) * TQ].T.astype(o_ref.dtype)


def _nsa_kernel(qn_ref, qr_ref, gt_ref, kc_ref, vct_ref, ks_ref, vst_ref, kw_ref, vwt_ref, o_ref,
                ms_ref, ls_ref, accs_ref, mw_ref, lw_ref, accw_ref):
    qi = pl.program_id(0)
    start = qi * TQ
    nc = kc_ref.shape[0]
    n_sel = nc // (NSA_SEL_LEN // NSA_CMP_STRIDE)
    t_row = start + lax.broadcasted_iota(I32, (1, TQ), 1)

    q4 = _stack_heads(qn_ref)
    sc = _dot_nt(kc_ref[...], q4)
    n_col = lax.broadcasted_iota(I32, (nc, TQ), 0)
    cmask = (n_col * NSA_CMP_STRIDE + (NSA_CMP_LEN - 1)) <= t_row
    pcs = []
    psum = jnp.zeros((nc, TQ), F32)
    for h in range(GROUP_HEADS):
        sm = jnp.where(cmask, sc[:, h * TQ:(h + 1) * TQ], NEG)
        e = jnp.exp2(sm - jnp.max(sm, axis=0, keepdims=True))
        p = jnp.where(cmask, e / jnp.sum(e, axis=0, keepdims=True), 0.0)
        psum = psum + p
        pcs.append(p.astype(BF16))
    o_cmp = _dot(vct_ref[...], jnp.concatenate(pcs, axis=1))

    jr = lax.broadcasted_iota(I32, (n_sel, nc), 0)
    nr = lax.broadcasted_iota(I32, (n_sel, nc), 1)
    ovt = jnp.where((nr >= 4 * jr - 1) & (nr <= 4 * jr + 3), 1.0, 0.0).astype(BF16)
    p_hi = psum.astype(BF16)
    p_lo = (psum - p_hi.astype(F32)).astype(BF16)
    imp = _dot(ovt, p_hi) + _dot(ovt, p_lo)
    jb = lax.broadcasted_iota(I32, (n_sel, TQ), 0)
    sel_shift = NSA_SEL_LEN.bit_length() - 1
    cur = t_row >> sel_shift
    forced = (jb == 0) | (jb == cur) | (jb == cur - 1)
    v = jnp.where(forced, jnp.inf, jnp.where(jb > cur, -jnp.inf, imp))
    jbf = jb.astype(F32)
    sel = jnp.zeros((n_sel, TQ), F32)
    for _ in range(min(NSA_SEL_TOP, n_sel)):
        mx = jnp.max(v, axis=0, keepdims=True)
        first = jnp.min(jnp.where(v == mx, jbf, float(n_sel)), axis=0, keepdims=True)
        hit = jbf == first
        sel = jnp.where(hit, 1.0, sel)
        v = jnp.where(hit, -jnp.inf, v)
    sel_b = sel.astype(BF16)

    qr4 = _stack_heads(qr_ref)
    _init_state(ms_ref, ls_ref, accs_ref)
    _init_state(mw_ref, lw_ref, accw_ref)
    kb_row = lax.broadcasted_iota(I32, (KB, TQ), 0)
    e_row = lax.broadcasted_iota(I32, (KB, n_sel), 0)
    e_col = lax.broadcasted_iota(I32, (KB, n_sel), 1)

    def sel_step(b, carry):
        kb = pl.multiple_of(b * KB, KB)
        s = _dot_nt(ks_ref[pl.ds(kb, KB), :], qr4)
        expand = jnp.where(((kb + e_row) >> sel_shift) == e_col, 1.0, 0.0).astype(BF16)
        chosen = _dot(expand, sel_b)
        mask = (chosen > 0.5) & ((kb + kb_row) <= t_row)
        _attend_block(s, mask, vst_ref[:, pl.ds(kb, KB)], ms_ref, ls_ref, accs_ref)
        return carry

    lax.fori_loop(0, (qi + KB // TQ) // (KB // TQ), sel_step, 0)

    k_row = lax.broadcasted_iota(I32, (TK, TQ), 0)
    for d in range(NSA_WINDOW // TK + 1):
        kt = qi - NSA_WINDOW // TK + d
        kb = pl.multiple_of(jnp.maximum(kt, 0) * TK, TK)
        kpos = kt * TK + k_row
        mask = (kpos <= t_row) & (kpos > t_row - NSA_WINDOW) & (kpos >= 0)
        s = _dot_nt(kw_ref[pl.ds(kb, TK), :], qr4)
        _attend_block(s, mask, vwt_ref[:, pl.ds(kb, TK)], mw_ref, lw_ref, accw_ref)

    g = lambda b: _lane_heads(lambda h: gt_ref[3 * h + b:3 * h + b + 1, :])
    ot = (g(0) * o_cmp + g(1) * (accs_ref[...] / ls_ref[...]) + g(2) * (accw_ref[...] / lw_ref[...]))
    _write_heads(o_ref, ot)


def _nsa_call(qn, qr, gt, kc, vct, ks, vst, kw, vwt):
    t = qn.shape[0]
    full = lambda a: pl.BlockSpec(a.shape, lambda i: (0,) * a.ndim)
    state = [pltpu.VMEM((1, GROUP_HEADS * TQ), F32), pltpu.VMEM((1, GROUP_HEADS * TQ), F32),
             pltpu.VMEM((HEAD_DIM, GROUP_HEADS * TQ), F32)]
    return pl.pallas_call(
        _nsa_kernel,
        grid=(t // TQ,),
        in_specs=[pl.BlockSpec((TQ, GROUP_WIDTH), lambda i: (i, 0)),
                  pl.BlockSpec((TQ, GROUP_WIDTH), lambda i: (i, 0)),
                  pl.BlockSpec((16, TQ), lambda i: (0, i)),
                  full(kc), full(vct), full(ks), full(vst), full(kw), full(vwt)],
        out_specs=pl.BlockSpec((TQ, GROUP_WIDTH), lambda i: (i, 0)),
        out_shape=jax.ShapeDtypeStruct((t, GROUP_WIDTH), BF16),
        scratch_shapes=state + state,
        compiler_params=_cparams("arbitrary"),
        name="nsa_attn",
    )(qn, qr, gt, kc, vct, ks, vst, kw, vwt)


INT_MIN = -2 ** 31
KEY_NEG_INF = -2139095041
DSA_SB = 256
COUNT_WAYS = 8


def _key_to_float(k):
    return lax.bitcast_convert_type(jnp.where(k >= 0, k, k ^ 0x7FFFFFFF), F32)


def _dsa_kernel(qd_ref, iq_ref, iwt_ref, kd_ref, vdt_ref, ik_ref, o_ref,
                sc_ref, m_ref, l_ref, acc_ref, cut_ref):
    qi = pl.program_id(0)
    start = qi * TQ
    nblk = (qi + KB // TQ) // (KB // TQ)
    t_total = kd_ref.shape[0]
    top = min(DSA_TOPK_MAX, t_total // 4)
    t_row = start + lax.broadcasted_iota(I32, (1, TQ), 1)
    k_row = lax.broadcasted_iota(I32, (KB, TQ), 0)
    s_row = lax.broadcasted_iota(I32, (DSA_SB, TQ), 0)
    iq_all = iq_ref[...].reshape(DSA_IDX_HEADS * TQ, DSA_IDX_DIM)
    iw = iwt_ref[...]

    def score_step(b, carry):
        for u in range(KB // DSA_SB):
            kb = pl.multiple_of(b * KB + u * DSA_SB, DSA_SB)
            lg = _dot_nt(ik_ref[pl.ds(kb, DSA_SB), :], iq_all)
            sc = jnp.zeros((DSA_SB, TQ), F32)
            for h in range(DSA_IDX_HEADS):
                sc = sc + iw[h:h + 1, :] * jnp.maximum(lg[:, h * TQ:(h + 1) * TQ], 0.0)
            sc_ref[pl.ds(kb, DSA_SB), :] = jnp.where((kb + s_row) <= t_row, sc, -jnp.inf)
        return carry

    lax.fori_loop(0, nblk, score_step, 0)

    def count(pred):
        def body(b, acc):
            kb = pl.multiple_of(b * KB, KB)
            hit = jnp.where(pred(sc_ref[pl.ds(kb, KB), :], kb), 1.0, 0.0)
            return acc + jnp.sum(hit.reshape(-1, COUNT_WAYS, SUBLANES, TQ), axis=0)
        part = lax.fori_loop(0, nblk, body, jnp.zeros((COUNT_WAYS, SUBLANES, TQ), F32))
        return jnp.sum(part.reshape(COUNT_WAYS * SUBLANES, TQ), axis=0, keepdims=True)

    ktop = float(top)
    key = jnp.where(count(lambda x, kb: x >= 0.0) >= ktop, 0, INT_MIN).astype(I32)

    def bit_step(b, key):
        cand = key | (1 << (30 - b))
        cand_f = _key_to_float(cand)
        return jnp.where(count(lambda x, kb: x >= cand_f) >= ktop, cand, key)

    key = lax.fori_loop(0, 31, bit_step, key)
    thr = jnp.where(key <= KEY_NEG_INF, -jnp.inf, _key_to_float(key))

    n_gt = count(lambda x, kb: x > thr)
    n_eq = count(lambda x, kb: x == thr)
    need = ktop - n_gt
    cut_ref[...] = jnp.full((1, TQ), t_total, I32)

    @pl.when(jnp.max(jnp.where((n_eq > need) & (need > 0.0) & (thr > -jnp.inf), 1.0, 0.0)) > 0.5)
    def _():
        nbits = max(1, (t_total - 1).bit_length())

        def idx_step(b, c):
            cc = c + (1 << (nbits - 1 - b))
            below = count(lambda x, kb: (x == thr) & ((kb + k_row) < cc))
            return jnp.where(below < need, cc, c)

        cut_ref[...] = lax.fori_loop(0, nbits, idx_step, jnp.zeros((1, TQ), I32))

    cut = cut_ref[...]

    q4 = _stack_heads(qd_ref)
    _init_state(m_ref, l_ref, acc_ref)

    def attn_step(b, carry):
        kb = pl.multiple_of(b * KB, KB)
        kpos = kb + k_row
        x = sc_ref[pl.ds(kb, KB), :]
        mask = ((x > thr) | ((x == thr) & (kpos <= cut))) & (kpos <= t_row)
        s = _dot_nt(kd_ref[pl.ds(kb, KB), :], q4)
        _attend_block(s, mask, vdt_ref[:, pl.ds(kb, KB)], m_ref, l_ref, acc_ref)
        return carry

    lax.fori_loop(0, nblk, attn_step, 0)
    _write_heads(o_ref, acc_ref[...] / l_ref[...])


def _dsa_call(qd, iq, iwt, kd, vdt, ik):
    t = qd.shape[0]
    full = lambda a: pl.BlockSpec(a.shape, lambda i: (0,) * a.ndim)
    return pl.pallas_call(
        _dsa_kernel,
        grid=(t // TQ,),
        in_specs=[pl.BlockSpec((TQ, GROUP_WIDTH), lambda i: (i, 0)),
                  pl.BlockSpec((DSA_IDX_HEADS, TQ, DSA_IDX_DIM), lambda i: (0, i, 0)),
                  pl.BlockSpec((DSA_IDX_HEADS, TQ), lambda i: (0, i)),
                  full(kd), full(vdt), full(ik)],
        out_specs=pl.BlockSpec((TQ, GROUP_WIDTH), lambda i: (i, 0)),
        out_shape=jax.ShapeDtypeStruct((t, GROUP_WIDTH), BF16),
        scratch_shapes=[pltpu.VMEM((t, TQ), F32),
                        pltpu.VMEM((1, GROUP_HEADS * TQ), F32), pltpu.VMEM((1, GROUP_HEADS * TQ), F32),
                        pltpu.VMEM((HEAD_DIM, GROUP_HEADS * TQ), F32),
                        pltpu.VMEM((1, TQ), I32)],
        compiler_params=_cparams("arbitrary"),
        name="dsa_attn",
    )(qd, iq, iwt, kd, vdt, ik)


def _outproj_kernel(ya_ref, yb_ref, yc_ref, yd_ref, x_ref, wo_ref, g1_ref, n2_ref, sc_ref, sh_ref,
                    wq_ref, sk_ref, x1_o, hn_o, s1_o, s2_o):
    gw = GROUP_WIDTH
    y = _dot(ya_ref[...], wo_ref[0:gw, :])
    y = y + _dot(yb_ref[...], wo_ref[gw:2 * gw, :])
    y = y + _dot(yc_ref[...], wo_ref[2 * gw:3 * gw, :])
    y = y + _dot(yd_ref[...], wo_ref[3 * gw:4 * gw, :])
    x1 = x_ref[...] + g1_ref[...] * y
    x1_o[...] = x1
    hn = (_rms(x1, n2_ref[...]) * (1.0 + sc_ref[...]) + sh_ref[...]).astype(BF16)
    hn_o[...] = hn
    q = _dot(hn, wq_ref[...]).astype(BF16)
    half = PEER_DKEY // 2
    for h in range(PEER_HEADS):
        s1_o[h] = _dot_nt(sk_ref[2 * h], q[:, (2 * h) * half:(2 * h + 1) * half])
        s2_o[h] = _dot_nt(sk_ref[2 * h + 1], q[:, (2 * h + 1) * half:(2 * h + 2) * half])


def _outproj_call(ya, yb, yc, yd, x, wo, g1, n2, sc, sh, wq, sk):
    t = x.shape[0]
    tm = min(256, t)
    tokb = lambda w: pl.BlockSpec((tm, w), lambda i: (i, 0))
    row = pl.BlockSpec((1, D_MODEL), lambda i: (0, 0))
    full = lambda a: pl.BlockSpec(a.shape, lambda i: (0,) * a.ndim)
    sspec = pl.BlockSpec((PEER_HEADS, PEER_NKEYS, tm), lambda i: (0, 0, i))
    sshape = jax.ShapeDtypeStruct((PEER_HEADS, PEER_NKEYS, t), F32)
    return pl.pallas_call(
        _outproj_kernel,
        grid=(t // tm,),
        in_specs=[tokb(512), tokb(512), tokb(512), tokb(512), tokb(D_MODEL), full(wo),
                  row, row, row, row, full(wq), full(sk)],
        out_specs=[tokb(D_MODEL), tokb(D_MODEL), sspec, sspec],
        out_shape=[jax.ShapeDtypeStruct((t, D_MODEL), F32), jax.ShapeDtypeStruct((t, D_MODEL), BF16),
                   sshape, sshape],
        compiler_params=_cparams("arbitrary"),
        name="outproj",
    )(ya, yb, yc, yd, x, wo, g1, n2, sc, sh, wq, sk)


def _top_values(v, k, out_ref):
    for r in range(k):
        mx = jnp.max(v, axis=0, keepdims=True)
        out_ref[r:r + 1, :] = mx
        v = jnp.where(v == mx, -jnp.inf, v)


PEER_VROWS = 24


def _peer_select_kernel(s1_ref, s2_ref, e1_o, e2_o, pthr_o, v1_ref, v2_ref):
    k = PEER_TOPK
    ts = s1_ref.shape[2]
    row8 = lax.broadcasted_iota(I32, (SUBLANES, ts), 0)
    v1_ref[...] = jnp.full((PEER_VROWS, ts), -jnp.inf, F32)
    v2_ref[...] = jnp.full((PEER_VROWS, ts), -jnp.inf, F32)
    for h in range(PEER_HEADS):
        s1 = s1_ref[h]
        s2 = s2_ref[h]
        _top_values(s1, k + 1, v1_ref)
        _top_values(s2, k + 1, v2_ref)
        v1 = v1_ref[...]
        v2 = v2_ref[...]
        x1 = jnp.exp(v1 - v1[0:1, :])
        x2 = jnp.exp(v2 - v2[0:1, :])
        pieces = [x1 * x2[0:1, :], x1[0:8, :] * x2[1:2, :]]
        for b, lim in ((2, 5), (3, 4), (4, 3), (5, 2), (6, 2), (7, 2)):
            pieces.append(jnp.where(row8 < lim, x1[0:8, :] * x2[b:b + 1, :], 0.0))
        pieces.append(x1[0:1, :] * x2[8:PEER_VROWS, :])
        cand = jnp.concatenate(pieces, axis=0)
        c = cand
        total = jnp.zeros((1, ts), F32)
        q16 = jnp.zeros((1, ts), F32)
        q17 = jnp.zeros((1, ts), F32)
        for _ in range(k + 1):
            mx = jnp.max(c, axis=0, keepdims=True)
            eq = c == mx
            new_total = total + jnp.sum(jnp.where(eq, 1.0, 0.0), axis=0, keepdims=True)
            q16 = jnp.where((total < k) & (new_total >= k), mx, q16)
            q17 = jnp.where((total < k + 1) & (new_total >= k + 1), mx, q17)
            total = new_total
            c = jnp.where(eq, -1.0, c)
        z = jnp.sum(jnp.where(cand >= q16, cand, 0.0), axis=0, keepdims=True)
        e1_o[:, h, :] = jnp.where(s1 >= v1[k - 1:k, :], jnp.exp(s1 - v1[0:1, :]) / z, 0.0)
        e2_o[h] = jnp.where(s2 >= v2[k - 1:k, :], jnp.exp(s2 - v2[0:1, :]), 0.0)
        pthr_o[h:h + 1, :] = 0.5 * (q16 + q17) / z


def _peer_select_call(s1, s2):
    t = s1.shape[2]
    ts = min(128, t)
    sspec = pl.BlockSpec((PEER_HEADS, PEER_NKEYS, ts), lambda i: (0, 0, i))
    return pl.pallas_call(
        _peer_select_kernel,
        grid=(t // ts,),
        in_specs=[sspec, sspec],
        out_specs=[pl.BlockSpec((PEER_NKEYS, PEER_HEADS, ts), lambda i: (0, 0, i)), sspec,
                   pl.BlockSpec((PEER_HEADS, ts), lambda i: (0, i))],
        out_shape=[jax.ShapeDtypeStruct((PEER_NKEYS, PEER_HEADS, t), F32),
                   jax.ShapeDtypeStruct((PEER_HEADS, PEER_NKEYS, t), F32),
                   jax.ShapeDtypeStruct((PEER_HEADS, t), F32)],
        scratch_shapes=[pltpu.VMEM((PEER_VROWS, ts), F32), pltpu.VMEM((PEER_VROWS, ts), F32)],
        compiler_params=_cparams("arbitrary"),
        name="peer_select",
    )(s1, s2)


PEER_TT = 512
PEER_TE = 1024
PEER_ROWS = PEER_TE // PEER_NKEYS
PEER_NCOL = D_MODEL // PEER_ROWS


def _peer_kernel(hn_ref, u_ref, v_ref, e1_ref, e2_ref, pthr_ref, o_ref, ht_ref, a0_ref, a1_ref):
    s = pl.program_id(1)
    tt = hn_ref.shape[0]

    @pl.when(s == 0)
    def _():
        o_ref[...] = jnp.zeros(o_ref.shape, F32)
        a1_ref[...] = jnp.zeros(a1_ref.shape, BF16)

    ht_ref[...] = _dot_nt(u_ref[...], hn_ref[...])

    def run(a_w, a_r):
        def trip(c, carry):
            r0 = pl.multiple_of(c * PEER_NKEYS, PEER_NKEYS)
            e1c = e1_ref[c]
            for lt in range(tt // LANES):
                cols = slice(lt * LANES, (lt + 1) * LANES)
                w = jnp.zeros((PEER_NKEYS, LANES), F32)
                for h in range(PEER_HEADS):
                    p = e1c[h:h + 1, cols] * e2_ref[h, :, cols]
                    w = w + jnp.where(p >= pthr_ref[h:h + 1, cols], p, 0.0)
                g = _gelu_tanh(ht_ref[pl.ds(r0, PEER_NKEYS), cols]) * w
                a_w[c, cols, :] = g.T.astype(BF16)
            a_prev = jnp.concatenate([a_r[k] for k in range(PEER_ROWS)], axis=1)
            o_ref[c] += _dot(a_prev, v_ref[c])
            return carry

        lax.fori_loop(0, PEER_ROWS, trip, 0)

    @pl.when(s % 2 == 0)
    def _():
        run(a0_ref, a1_ref)

    @pl.when(s % 2 == 1)
    def _():
        run(a1_ref, a0_ref)


def _peer_call(hn, u, vc, e1, e2, pthr):
    t = hn.shape[0]
    tt = min(PEER_TT, t)
    n_tiles = u.shape[0] // PEER_TE
    cur = lambda s: jnp.minimum(s, n_tiles - 1)
    prev = lambda s: jnp.maximum(s - 1, 0)
    return pl.pallas_call(
        _peer_kernel,
        grid=(t // tt, n_tiles + 1),
        in_specs=[pl.BlockSpec((tt, D_MODEL), lambda i, s: (i, 0)),
                  pl.BlockSpec((PEER_TE, D_MODEL), lambda i, s: (cur(s), 0)),
                  pl.BlockSpec((PEER_ROWS, PEER_TE, PEER_NCOL), lambda i, s: (0, prev(s), 0)),
                  pl.BlockSpec((PEER_ROWS, PEER_HEADS, tt), lambda i, s: (cur(s), 0, i)),
                  pl.BlockSpec((PEER_HEADS, PEER_NKEYS, tt), lambda i, s: (0, 0, i)),
                  pl.BlockSpec((PEER_HEADS, tt), lambda i, s: (0, i))],
        out_specs=pl.BlockSpec((PEER_ROWS, tt, PEER_NCOL), lambda i, s: (0, i, 0)),
        out_shape=jax.ShapeDtypeStruct((PEER_ROWS, t, PEER_NCOL), F32),
        scratch_shapes=[pltpu.VMEM((PEER_TE, tt), F32),
                        pltpu.VMEM((PEER_ROWS, tt, PEER_NKEYS), BF16),
                        pltpu.VMEM((PEER_ROWS, tt, PEER_NKEYS), BF16)],
        compiler_params=_cparams("arbitrary", "arbitrary"),
        name="peer_experts",
    )(hn, u, vc, e1, e2, pthr)


def _residual_kernel(x_ref, p_ref, g_ref, o_ref):
    for c in range(PEER_ROWS):
        cols = slice(c * PEER_NCOL, (c + 1) * PEER_NCOL)
        o_ref[:, cols] = x_ref[:, cols] + g_ref[:, cols] * p_ref[c]


def _residual_call(x1, p, g2):
    t = x1.shape[0]
    tb = min(512, t)
    tok = pl.BlockSpec((tb, D_MODEL), lambda i: (i, 0))
    return pl.pallas_call(
        _residual_kernel,
        grid=(t // tb,),
        in_specs=[tok, pl.BlockSpec((PEER_ROWS, tb, PEER_NCOL), lambda i: (0, i, 0)),
                  pl.BlockSpec((1, D_MODEL), lambda i: (0, 0))],
        out_specs=tok,
        out_shape=jax.ShapeDtypeStruct((t, D_MODEL), F32),
        compiler_params=_cparams("arbitrary"),
        name="peer_residual",
    )(x1, p, g2)


def _rope_tables(positions, rot_dim, period):
    half = rot_dim // 2
    inv = ROPE_THETA ** (-jnp.arange(half, dtype=F32) / half)
    ang = positions.astype(F32)[:, None] * inv
    cos, sin = jnp.cos(ang), jnp.sin(ang)
    t = positions.shape[0]
    pad1 = jnp.ones((t, period - rot_dim), F32)
    pad0 = jnp.zeros((t, period - rot_dim), F32)
    cs = jnp.concatenate([cos, cos, pad1], axis=1)
    sn = jnp.concatenate([-sin, sin, pad0], axis=1)
    reps = LANES // period
    return jnp.tile(cs, (1, reps)), jnp.tile(sn, (1, reps))


def _pad_w_in(w):
    assert w.shape[1] == IN_COLS
    return jnp.pad(w.astype(BF16), ((0, 0), (0, N_COLS - IN_COLS)))


def kernel(x, c, positions, ada_w, ada_b, norm1_g, norm2_g, w_in, w_out, conv_w, conv_b, conv_ln_g, conv_ln_b,
           sgu_ln_g, sgu_ln_b, sgu_w, sgu_b, nsa_q_g, nsa_k_g, nsa_cmp_pos, nsa_cmp_w1, nsa_cmp_w2, dsa_q_g,
           dsa_k_g, peer_wq, peer_subkeys, peer_u, peer_v):
    assert x.shape[0] == 1 and c.shape[0] == 1
    n_layers = ada_w.shape[0]
    xs = x[0]
    t = xs.shape[0]
    assert t % 1024 == 0
    pos = positions[0]
    cs, sn = _rope_tables(pos, HEAD_DIM // 4, LANES)
    csi, sni = _rope_tables(pos, DSA_IDX_DIM // 4, DSA_IDX_DIM)
    ada = _ada_call(c, ada_w, ada_b)
    row = lambda v: v.reshape(1, -1)
    nc = t // NSA_CMP_STRIDE

    for i in range(n_layers):
        sh1, sc1, g1, sh2, sc2, g2 = [ada[i, :, k * D_MODEL:(k + 1) * D_MODEL] for k in range(6)]
        proj = _inproj_call(xs, row(norm1_g[i]), sc1, sh1, _pad_w_in(w_in[i]))

        cw = jnp.concatenate([conv_w[i], jnp.zeros((CONV_HALO - CONV_WIDTH, GROUP_WIDTH), F32)], axis=0)
        sb = jnp.repeat(sgu_b[i].T, HEAD_DIM, axis=1)
        ya, yb = _mixer_call(proj, cw, row(conv_b[i]), row(conv_ln_g[i]), row(conv_ln_b[i]),
                             row(sgu_ln_g[i]), row(sgu_ln_b[i]), sgu_w[i], sb)

        (qn, qr, qd, iq, kcmp, vcmp, ks, vst, kw, vwt, kd, vdt, ik, gt, iwt) = _prep_call(
            proj, cs, sn, csi, sni, row(nsa_q_g[i]), nsa_k_g[i], row(dsa_q_g[i]), row(dsa_k_g[i]))

        posb = jnp.broadcast_to(nsa_cmp_pos[i].reshape(2, 1, NSA_CMP_LEN * HEAD_DIM),
                                (2, SUBLANES, NSA_CMP_LEN * HEAD_DIM)).astype(BF16)
        kc, vct = _compress_call(kcmp.reshape(nc, NSA_CMP_STRIDE * HEAD_DIM),
                                 vcmp.reshape(nc, NSA_CMP_STRIDE * HEAD_DIM),
                                 nsa_cmp_w1[i].astype(BF16), nsa_cmp_w2[i].astype(BF16), posb,
                                 nsa_k_g[i, 0:1])
        yc = _nsa_call(qn, qr, gt, kc, vct, ks, vst, kw, vwt)
        yd = _dsa_call(qd, iq, iwt, kd, vdt, ik)

        sk = peer_subkeys[i].reshape(2 * PEER_HEADS, PEER_NKEYS, PEER_DKEY // 2).astype(BF16)
        x1, hn2, s1, s2 = _outproj_call(ya, yb, yc, yd, xs, w_out[i].astype(BF16), g1, row(norm2_g[i]),
                                        sc2, sh2, peer_wq[i].astype(BF16), sk)
        e1, e2, pthr = _peer_select_call(s1, s2)
        vc = peer_v[i].astype(BF16).reshape(-1, PEER_ROWS, PEER_NCOL).transpose(1, 0, 2)
        pe = _peer_call(hn2, peer_u[i].astype(BF16), vc, e1, e2, pthr)
        xs = _residual_call(x1, pe, g2)
    return xs[None]
```
